```python
import jax, jax.numpy as jnp
from jax import lax
import numpy as np

D_MODEL = 1024
BATCH = 16
SEQ = 2048
DEPTH = 1
DEC_BATCH = 1
DEC_SEQ = 16384
PAST_LEN = 128

N_META = 16
GRID_W = 64
Q_BLOCK = 128
ROPE_THETA = 10000.0
EPS = 1e-6

HEADS_A = 8
KV_HEADS_A = 2
HEAD_DIM_A = 128
WIDTH_A = HEADS_A * HEAD_DIM_A

HEADS_B = 8
Q_LORA = 384
KV_LORA = 256
NOPE_B = 128
ROPE_B = 64
V_B = 128
QK_B = NOPE_B + ROPE_B
WIDTH_B = HEADS_B * V_B

IN_SIZES = (WIDTH_A, KV_HEADS_A * HEAD_DIM_A, KV_HEADS_A * HEAD_DIM_A, WIDTH_A,
            Q_LORA, KV_LORA, ROPE_B, WIDTH_B, D_MODEL, D_MODEL)
IN_COLS = sum(IN_SIZES)
SPLIT_IDX = tuple(sum(IN_SIZES[:i + 1]) for i in range(len(IN_SIZES) - 1))

kernel_name = "hybrid_gqa_mla_gated_encoder"


def rms_norm(x, g):
    xf = x.astype(jnp.float32)
    y = xf * lax.rsqrt(jnp.mean(xf * xf, axis=-1, keepdims=True) + EPS)
    return (y * g.astype(jnp.float32)).astype(x.dtype)


def axial_rope_tables(rows, rot_dim):
    axis_dim = rot_dim // 2
    inv_freq = ROPE_THETA ** (-jnp.arange(0, axis_dim, 2, dtype=jnp.float32) / axis_dim)
    row = jnp.repeat(jnp.arange(rows, dtype=jnp.float32), GRID_W)
    col = jnp.tile(jnp.arange(GRID_W, dtype=jnp.float32), rows)
    pad = jnp.zeros((N_META,), jnp.float32)
    row = jnp.concatenate([pad, row])
    col = jnp.concatenate([pad, col])
    ang_r = row[:, None] * inv_freq[None, :]
    ang_c = col[:, None] * inv_freq[None, :]
    return (jnp.cos(ang_r), jnp.sin(ang_r), jnp.cos(ang_c), jnp.sin(ang_c))


def _rotate(y, c, s):
    half = y.shape[-1] // 2
    y1, y2 = y[..., :half], y[..., half:]
    c = c[None, :, None, :]
    s = s[None, :, None, :]
    return jnp.concatenate([y1 * c - y2 * s, y1 * s + y2 * c], axis=-1)


def apply_axial_rope(x, tabs):
    cr, sr, cc, sc = tabs
    d = x.shape[-1] // 2
    xf = x.astype(jnp.float32)
    out = jnp.concatenate([_rotate(xf[..., :d], cr, sr), _rotate(xf[..., d:], cc, sc)], axis=-1)
    return out.astype(x.dtype)


def _attend(q, k, v):
    scale = q.shape[-1] ** -0.5
    s = jnp.einsum('bqhgd,bkhd->bhgqk', q.astype(jnp.float32), k.astype(jnp.float32)) * scale
    p = jax.nn.softmax(s, axis=-1)
    return jnp.einsum('bhgqk,bkhd->bqhgd', p.astype(v.dtype), v)


def bidir_attention(q, k, v):
    b, l, hk, g, dk = q.shape
    dv = v.shape[-1]
    n = l - N_META
    out_meta = _attend(q[:, :N_META], k, v)
    qb = jnp.moveaxis(q[:, N_META:].reshape(b, n // Q_BLOCK, Q_BLOCK, hk, g, dk), 1, 0)
    ob = lax.map(lambda blk: _attend(blk, k, v), qb)
    ob = jnp.moveaxis(ob, 0, 1).reshape(b, n, hk, g, dv)
    return jnp.concatenate([out_meta, ob], axis=1).reshape(b, l, hk * g * dv)


def mixer_layer(h, rope_a, rope_b, norm_gain, w_in, b_merge, q_norm_a, k_norm_a,
                cq_norm, ckv_norm, w_uq, w_ukv, q_norm_b, k_norm_b,
                w_proj_a, w_proj_b, w_out):
    b, l, _ = h.shape
    u = rms_norm(h, norm_gain)
    z = u @ w_in
    qa, ka, va, ga, cq, ckv, kr, gb, ma, mb = jnp.split(z, SPLIT_IDX, axis=-1)

    qa = apply_axial_rope(rms_norm(qa.reshape(b, l, HEADS_A, HEAD_DIM_A), q_norm_a), rope_a)
    ka = apply_axial_rope(rms_norm(ka.reshape(b, l, KV_HEADS_A, HEAD_DIM_A), k_norm_a), rope_a)
    va = va.reshape(b, l, KV_HEADS_A, HEAD_DIM_A)
    qa = qa.reshape(b, l, KV_HEADS_A, HEADS_A // KV_HEADS_A, HEAD_DIM_A)
    oa = bidir_attention(qa, ka, va)
    ya = (oa * jax.nn.silu(ga)) @ w_proj_a

    qb = (rms_norm(cq, cq_norm) @ w_uq).reshape(b, l, HEADS_B, QK_B)
    kvb = (rms_norm(ckv, ckv_norm) @ w_ukv).reshape(b, l, HEADS_B, NOPE_B + V_B)
    k_nope, vb = kvb[..., :NOPE_B], kvb[..., NOPE_B:]
    kb = jnp.concatenate([k_nope, jnp.broadcast_to(kr[:, :, None, :], (b, l, HEADS_B, ROPE_B))], axis=-1)
    qb = rms_norm(qb, q_norm_b)
    kb = rms_norm(kb, k_norm_b)
    qb = jnp.concatenate([qb[..., :NOPE_B], apply_axial_rope(qb[..., NOPE_B:], rope_b)], axis=-1)
    kb = jnp.concatenate([kb[..., :NOPE_B], apply_axial_rope(kb[..., NOPE_B:], rope_b)], axis=-1)
    ob = bidir_attention(qb[:, :, :, None, :], kb, vb)
    yb = (ob * jax.nn.silu(gb)) @ w_proj_b

    gate_a = jax.nn.sigmoid(ma + b_merge[0])
    gate_b = jax.nn.sigmoid(mb + b_merge[1])
    return h + (gate_a * ya + gate_b * yb) @ w_out


def encode(x, meta_tokens, norm_gain, w_in, b_merge, q_norm_a, k_norm_a, cq_norm, ckv_norm,
           w_uq, w_ukv, q_norm_b, k_norm_b, w_proj_a, w_proj_b, w_out):
    b, n, _ = x.shape
    rows = n // GRID_W
    rope_a = axial_rope_tables(rows, HEAD_DIM_A)
    rope_b = axial_rope_tables(rows, ROPE_B)
    meta = jnp.broadcast_to(meta_tokens[None].astype(x.dtype), (b, N_META, D_MODEL))
    h = jnp.concatenate([meta, x], axis=1)
    for i in range(DEPTH):
        h = mixer_layer(h, rope_a, rope_b, norm_gain[i], w_in[i], b_merge[i], q_norm_a[i], k_norm_a[i],
                        cq_norm[i], ckv_norm[i], w_uq[i], w_ukv[i], q_norm_b[i], k_norm_b[i],
                        w_proj_a[i], w_proj_b[i], w_out[i])
    return h[:, N_META:]


def setup_inputs(seed: int = 0) -> dict:
    key = jax.random.key(seed)
    ks = jax.random.split(key, 18)
    f32 = jnp.float32

    def nrm(k, shape, scale):
        return jax.random.normal(k, shape, f32) * scale

    def gain(k, shape):
        return 1.0 + 0.02 * jax.random.normal(k, shape, f32)

    return {
        "x_prompt": nrm(ks[0], (BATCH, SEQ, D_MODEL), 1.0),
        "x_sample": nrm(ks[1], (DEC_BATCH, DEC_SEQ, D_MODEL), 1.0),
        "meta_tokens": nrm(ks[2], (N_META, D_MODEL), 1.0),
        "norm_gain": gain(ks[3], (DEPTH, D_MODEL)),
        "w_in": nrm(ks[4], (DEPTH, D_MODEL, IN_COLS), D_MODEL ** -0.5),
        "b_merge": nrm(ks[5], (DEPTH, 2, D_MODEL), 0.1),
        "q_norm_a": gain(ks[6], (DEPTH, HEAD_DIM_A)),
        "k_norm_a": gain(ks[7], (DEPTH, HEAD_DIM_A)),
        "cq_norm": gain(ks[8], (DEPTH, Q_LORA)),
        "ckv_norm": gain(ks[9], (DEPTH, KV_LORA)),
        "w_uq": nrm(ks[10], (DEPTH, Q_LORA, HEADS_B * QK_B), Q_LORA ** -0.5),
        "w_ukv": nrm(ks[11], (DEPTH, KV_LORA, HEADS_B * (NOPE_B + V_B)), KV_LORA ** -0.5),
        "q_norm_b": gain(ks[12], (DEPTH, QK_B)),
        "k_norm_b": gain(ks[13], (DEPTH, QK_B)),
        "w_proj_a": nrm(ks[14], (DEPTH, WIDTH_A, D_MODEL), WIDTH_A ** -0.5),
        "w_proj_b": nrm(ks[15], (DEPTH, WIDTH_B, D_MODEL), WIDTH_B ** -0.5),
        "w_out": nrm(ks[16], (DEPTH, D_MODEL, D_MODEL), D_MODEL ** -0.5),
    }


def reference(x_prompt, x_sample, meta_tokens, norm_gain, w_in, b_merge, q_norm_a, k_norm_a,
              cq_norm, ckv_norm, w_uq, w_ukv, q_norm_b, k_norm_b, w_proj_a, w_proj_b, w_out):
    y_prompt = encode(x_prompt, meta_tokens, norm_gain, w_in, b_merge, q_norm_a, k_norm_a, cq_norm, ckv_norm,
                      w_uq, w_ukv, q_norm_b, k_norm_b, w_proj_a, w_proj_b, w_out)
    y_sample = encode(x_sample, meta_tokens, norm_gain, w_in, b_merge, q_norm_a, k_norm_a, cq_norm, ckv_norm,
                      w_uq, w_ukv, q_norm_b, k_norm_b, w_proj_a, w_proj_b, w_out)
    return (y_prompt, y_sample)
```

```python
import functools

import jax
import jax.numpy as jnp
from jax import lax
from jax.experimental import pallas as pl
from jax.experimental.pallas import tpu as pltpu

F32 = jnp.float32
BF16 = jnp.bfloat16

D_MODEL = 1024
N_META = 16
GRID_W = 64
ROPE_THETA = 10000.0
EPS = 1e-6

HEADS_A = 8
KV_HEADS_A = 2
HEAD_DIM_A = 128
GROUP_A = HEADS_A // KV_HEADS_A

HEADS_B = 8
Q_LORA = 384
KV_LORA = 256
NOPE_B = 128
ROPE_B = 64
V_B = 128
QK_B = NOPE_B + ROPE_B

LANES = 128
QK_B_PAD = 2 * LANES
META_PAD = LANES
VMEM_LIMIT_BYTES = 56 * 1024 * 1024

C_QA = 0
C_KA = C_QA + HEADS_A * HEAD_DIM_A
C_VA = C_KA + KV_HEADS_A * HEAD_DIM_A
C_CQ = C_VA + KV_HEADS_A * HEAD_DIM_A
C_CKV = C_CQ + Q_LORA
C_KR = C_CKV + KV_LORA
C_GATES = C_KR + LANES
N_GATES = 4 * D_MODEL
C_END = C_GATES + N_GATES

NEG_BIG = -1e30


def _rms_scale(x, dim):
    return lax.rsqrt(jnp.sum(x * x, axis=-1, keepdims=True) * (1.0 / dim) + EPS)


def _rope(y, tab_ref, shift):
    up = pltpu.roll(y, LANES - shift, 1)
    down = pltpu.roll(y, shift, 1)
    return y * tab_ref[0] + up * tab_ref[1] + down * tab_ref[2]


def _proj_kernel(x_ref, gain_ref, w_in_ref, w_uq_ref, w_ukv_ref, qna_ref, kna_ref, cqn_ref, ckvn_ref,
                 qnb_ref, knb_ref, rope_a_ref, rope_b_ref,
                 qa_ref, ka_ref, va_ref, qb_ref, kb_ref, vb_ref, gates_ref):
    x = x_ref[0]
    u = (x * _rms_scale(x, D_MODEL) * gain_ref[...]).astype(BF16)

    def proj(c0, c1):
        return jnp.dot(u, w_in_ref[:, c0:c1], preferred_element_type=F32)

    gates_ref[0] = proj(C_GATES, C_END)

    zq = proj(C_QA, C_KA)
    gq = qna_ref[...] * (HEAD_DIM_A ** -0.5)
    for h in range(HEADS_A):
        xh = zq[:, h * LANES:(h + 1) * LANES]
        y = xh * _rms_scale(xh, HEAD_DIM_A) * gq
        qa_ref[0, h] = _rope(y, rope_a_ref, HEAD_DIM_A // 4).astype(BF16)
    zk = proj(C_KA, C_VA)
    for h in range(KV_HEADS_A):
        xh = zk[:, h * LANES:(h + 1) * LANES]
        y = xh * _rms_scale(xh, HEAD_DIM_A) * kna_ref[...]
        ka_ref[0, h] = _rope(y, rope_a_ref, HEAD_DIM_A // 4).astype(BF16)
    zv = proj(C_VA, C_CQ)
    for h in range(KV_HEADS_A):
        va_ref[0, h] = zv[:, h * LANES:(h + 1) * LANES].astype(BF16)

    cq = proj(C_CQ, C_CKV)
    cqn = (cq * _rms_scale(cq, Q_LORA) * cqn_ref[...]).astype(BF16)
    zqb = jnp.dot(cqn, w_uq_ref[...], preferred_element_type=F32)
    gqb = qnb_ref[...] * (QK_B ** -0.5)
    for h in range(HEADS_B):
        xh = zqb[:, h * QK_B_PAD:(h + 1) * QK_B_PAD]
        y = xh * _rms_scale(xh, QK_B) * gqb
        qb_ref[0, h, :, :LANES] = y[:, :LANES].astype(BF16)
        qb_ref[0, h, :, LANES:] = _rope(y[:, LANES:], rope_b_ref, ROPE_B // 4).astype(BF16)

    ckv = proj(C_CKV, C_KR)
    ckvn = (ckv * _rms_scale(ckv, KV_LORA) * ckvn_ref[...]).astype(BF16)
    zkv = jnp.dot(ckvn, w_ukv_ref[...], preferred_element_type=F32)
    kr = proj(C_KR, C_GATES)
    kr_ss = jnp.sum(kr * kr, axis=-1, keepdims=True)
    gk_nope = knb_ref[:, :LANES]
    gk_rope = knb_ref[:, LANES:]
    for h in range(HEADS_B):
        kn = zkv[:, h * 2 * LANES:h * 2 * LANES + LANES]
        ss = jnp.sum(kn * kn, axis=-1, keepdims=True) + kr_ss
        r = lax.rsqrt(ss * (1.0 / QK_B) + EPS)
        kb_ref[0, h, :, :LANES] = (kn * r * gk_nope).astype(BF16)
        kb_ref[0, h, :, LANES:] = _rope(kr * r * gk_rope, rope_b_ref, ROPE_B // 4).astype(BF16)
        vb_ref[0, h] = zkv[:, h * 2 * LANES + LANES:(h + 1) * 2 * LANES].astype(BF16)


def _const_spec(shape):
    return pl.BlockSpec(shape, lambda *_: (0,) * len(shape))


def _project(x, tm, gain, w_in, w_uq, w_ukv, qna, kna, cqn, ckvn, qnb, knb, rope_a, rope_b):
    b, n, _ = x.shape
    grid = (b, n // tm)

    def head_out(heads, width):
        return (jax.ShapeDtypeStruct((b, heads, n, width), BF16),
                pl.BlockSpec((1, heads, tm, width), lambda bi, i: (bi, 0, i, 0)))

    outs = [head_out(HEADS_A, HEAD_DIM_A), head_out(KV_HEADS_A, HEAD_DIM_A), head_out(KV_HEADS_A, HEAD_DIM_A),
            head_out(HEADS_B, QK_B_PAD), head_out(HEADS_B, QK_B_PAD), head_out(HEADS_B, V_B),
            (jax.ShapeDtypeStruct((b, n, N_GATES), F32), pl.BlockSpec((1, tm, N_GATES), lambda bi, i: (bi, i, 0)))]
    consts = (gain, w_in, w_uq, w_ukv, qna, kna, cqn, ckvn, qnb, knb)
    rope_spec = pl.BlockSpec((3, tm, LANES), lambda bi, i: (0, i, 0))
    return pl.pallas_call(
        _proj_kernel,
        grid=grid,
        in_specs=[pl.BlockSpec((1, tm, D_MODEL), lambda bi, i: (bi, i, 0))]
        + [_const_spec(c.shape) for c in consts] + [rope_spec, rope_spec],
        out_specs=[o[1] for o in outs],
        out_shape=[o[0] for o in outs],
        compiler_params=pltpu.CompilerParams(
            dimension_semantics=("arbitrary", "arbitrary"), vmem_limit_bytes=VMEM_LIMIT_BYTES),
        name="in_proj",
    )(x, *consts, rope_a, rope_b)


def _attn_kernel(q_ref, k_ref, v_ref, km_ref, vm_ref, g_ref, o_ref, m_sc, l_sc, acc_sc, *, group, tq, tk, nk):
    m_rows = group * tq
    q = q_ref[0].reshape(m_rows, q_ref.shape[-1])
    nt = (((1,), (1,)), ((), ()))

    s = lax.dot_general(q, km_ref[0], nt, preferred_element_type=F32)
    col = lax.broadcasted_iota(jnp.int32, s.shape, 1)
    s = jnp.where(col < N_META, s, NEG_BIG)
    m0 = jnp.max(s, axis=1, keepdims=True)
    p = jnp.exp(s - m0)
    m_sc[...] = jnp.broadcast_to(m0, m_sc.shape)
    l_sc[...] = jnp.broadcast_to(jnp.sum(p, axis=1, keepdims=True), l_sc.shape)
    acc_sc[...] = jnp.dot(p.astype(BF16), vm_ref[0], preferred_element_type=F32)

    def body(j, carry):
        off = pl.multiple_of(j * tk, tk)
        k = k_ref[0, 0, pl.ds(off, tk), :]
        v = v_ref[0, 0, pl.ds(off, tk), :]
        s = lax.dot_general(q, k, nt, preferred_element_type=F32)
        m_prev = m_sc[...]
        m_new = jnp.maximum(m_prev, jnp.max(s, axis=1, keepdims=True))
        alpha = jnp.exp(m_prev - m_new)
        p = jnp.exp(s - jnp.concatenate([m_new] * (tk // LANES), axis=1))
        l_sc[...] = alpha * l_sc[...] + jnp.sum(p, axis=1, keepdims=True)
        acc_sc[...] = alpha * acc_sc[...] + jnp.dot(p.astype(BF16), v, preferred_element_type=F32)
        m_sc[...] = m_new
        return carry

    lax.fori_loop(0, nk, body, 0)

    o = acc_sc[...] / l_sc[...]
    for h in range(group):
        g = g_ref[0, :, h * LANES:(h + 1) * LANES]
        o_ref[0, :, h * LANES:(h + 1) * LANES] = (o[h * tq:(h + 1) * tq] * (g * jax.nn.sigmoid(g))).astype(BF16)


def _attention(q, k, v, k_meta, v_meta, gates, *, group, gate_col, tq, tk):
    b, hq, n, dk = q.shape
    hkv = k.shape[1]
    dv = v.shape[-1]
    width = group * dv
    gate_blk = gate_col // width
    kernel = functools.partial(_attn_kernel, group=group, tq=tq, tk=tk, nk=n // tk)
    return pl.pallas_call(
        kernel,
        grid=(b, hkv, n // tq),
        in_specs=[
            pl.BlockSpec((1, group, tq, dk), lambda bi, g, i: (bi, g, i, 0)),
            pl.BlockSpec((1, 1, n, dk), lambda bi, g, i: (bi, g, 0, 0)),
            pl.BlockSpec((1, 1, n, dv), lambda bi, g, i: (bi, g, 0, 0)),
            pl.BlockSpec((1, META_PAD, dk), lambda bi, g, i: (g, 0, 0)),
            pl.BlockSpec((1, META_PAD, dv), lambda bi, g, i: (g, 0, 0)),
            pl.BlockSpec((1, tq, width), lambda bi, g, i: (bi, i, gate_blk + g)),
        ],
        out_specs=pl.BlockSpec((1, tq, width), lambda bi, g, i: (bi, i, g)),
        out_shape=jax.ShapeDtypeStruct((b, n, hq * dv), BF16),
        scratch_shapes=[pltpu.VMEM((group * tq, LANES), F32)] * 2 + [pltpu.VMEM((group * tq, dv), F32)],
        compiler_params=pltpu.CompilerParams(
            dimension_semantics=("arbitrary", "arbitrary", "arbitrary"), vmem_limit_bytes=VMEM_LIMIT_BYTES),
        name="attn_g%d" % group,
    )(q, k, v, k_meta, v_meta, gates)


def _out_kernel(x_ref, oa_ref, ob_ref, ma_ref, mb_ref, bm_ref, wpa_ref, wpb_ref, wo_ref, y_ref):
    ya = jnp.dot(oa_ref[0], wpa_ref[...], preferred_element_type=F32)
    yb = jnp.dot(ob_ref[0], wpb_ref[...], preferred_element_type=F32)
    mix = jax.nn.sigmoid(ma_ref[0] + bm_ref[0:1]) * ya + jax.nn.sigmoid(mb_ref[0] + bm_ref[1:2]) * yb
    y_ref[0] = x_ref[0] + jnp.dot(mix.astype(BF16), wo_ref[...], preferred_element_type=F32)


def _output(x, oa, ob, gates, b_merge, w_proj_a, w_proj_b, w_out, tm):
    b, n, _ = x.shape
    tok = lambda c: pl.BlockSpec((1, tm, D_MODEL), lambda bi, i: (bi, i, c))
    return pl.pallas_call(
        _out_kernel,
        grid=(b, n // tm),
        in_specs=[tok(0), tok(0), tok(0), tok(2), tok(3), _const_spec(b_merge.shape),
                  _const_spec(w_proj_a.shape), _const_spec(w_proj_b.shape), _const_spec(w_out.shape)],
        out_specs=tok(0),
        out_shape=jax.ShapeDtypeStruct(x.shape, F32),
        compiler_params=pltpu.CompilerParams(
            dimension_semantics=("arbitrary", "arbitrary"), vmem_limit_bytes=VMEM_LIMIT_BYTES),
        name="out_proj",
    )(x, oa, ob, gates, gates, b_merge, w_proj_a, w_proj_b, w_out)


def _rope_tables(n, rot_dim):
    rows = n // GRID_W
    axis_dim = rot_dim // 2
    inv_freq = ROPE_THETA ** (-jnp.arange(0, axis_dim, 2, dtype=F32) / axis_dim)
    row = jnp.repeat(jnp.arange(rows, dtype=F32), GRID_W)
    col = jnp.tile(jnp.arange(GRID_W, dtype=F32), rows)
    ang_r = row[:, None] * inv_freq[None, :]
    ang_c = col[:, None] * inv_freq[None, :]
    cr, sr, cc, sc = jnp.cos(ang_r), jnp.sin(ang_r), jnp.cos(ang_c), jnp.sin(ang_c)
    z = jnp.zeros_like(cr)
    tabs = jnp.stack([jnp.concatenate([cr, cr, cc, cc], axis=1),
                      jnp.concatenate([-sr, z, -sc, z], axis=1),
                      jnp.concatenate([z, sr, z, sc], axis=1)])
    return jnp.pad(tabs, ((0, 0), (0, 0), (0, LANES - rot_dim)))


def _identity_rope(n, rot_dim):
    cos = jnp.pad(jnp.ones((n, rot_dim), F32), ((0, 0), (0, LANES - rot_dim)))
    return jnp.stack([cos, jnp.zeros_like(cos), jnp.zeros_like(cos)])


def _pad_meta(a):
    return jnp.pad(a[0], ((0, 0), (0, META_PAD - N_META), (0, 0)))


def _tile(n, pref):
    return pref if n % pref == 0 else n


def kernel(x_prompt, x_sample, meta_tokens, norm_gain, w_in, b_merge, q_norm_a, k_norm_a, cq_norm, ckv_norm,
           w_uq, w_ukv, q_norm_b, k_norm_b, w_proj_a, w_proj_b, w_out):
    assert norm_gain.shape[0] == 1, "meta-token queries are only skipped for a single layer"
    wi = w_in[0]
    s0 = HEADS_A * HEAD_DIM_A
    s1 = s0 + 2 * KV_HEADS_A * HEAD_DIM_A
    s2 = s1 + HEADS_A * HEAD_DIM_A
    s3 = s2 + Q_LORA + KV_LORA
    s4 = s3 + ROPE_B
    w_in_p = jnp.concatenate(
        [wi[:, :s1], wi[:, s2:s4], jnp.zeros((D_MODEL, LANES - ROPE_B), wi.dtype), wi[:, s1:s2], wi[:, s4:]],
        axis=1).astype(BF16)
    w_uq_p = jnp.pad(w_uq[0].reshape(Q_LORA, HEADS_B, QK_B), ((0, 0), (0, 0), (0, QK_B_PAD - QK_B))
                     ).reshape(Q_LORA, HEADS_B * QK_B_PAD).astype(BF16)
    w_ukv_b = w_ukv[0].astype(BF16)
    row = lambda a: a.reshape(1, -1).astype(F32)
    pad_b = lambda a: jnp.pad(row(a), ((0, 0), (0, QK_B_PAD - QK_B)))
    norms = (row(q_norm_a[0]), row(k_norm_a[0]), row(cq_norm[0]), row(ckv_norm[0]),
             pad_b(q_norm_b[0]), pad_b(k_norm_b[0]))
    gain = row(norm_gain[0])
    proj_w = (gain, w_in_p, w_uq_p, w_ukv_b) + norms
    wpa, wpb, wo = w_proj_a[0].astype(BF16), w_proj_b[0].astype(BF16), w_out[0].astype(BF16)
    bm = b_merge[0].astype(F32)

    meta = _project(meta_tokens[None].astype(F32), N_META, *proj_w,
                    _identity_rope(N_META, HEAD_DIM_A), _identity_rope(N_META, ROPE_B))
    ka_m, va_m, kb_m, vb_m = (_pad_meta(meta[i]) for i in (1, 2, 4, 5))

    def encode(x):
        n = x.shape[1]
        qa, ka, va, qb, kb, vb, gates = _project(
            x, _tile(n, 256), *proj_w, _rope_tables(n, HEAD_DIM_A), _rope_tables(n, ROPE_B))
        oa = _attention(qa, ka, va, ka_m, va_m, gates, group=GROUP_A, gate_col=0,
                        tq=_tile(n, 128), tk=_tile(n, 512))
        ob = _attention(qb, kb, vb, kb_m, vb_m, gates, group=1, gate_col=D_MODEL,
                        tq=_tile(n, 512), tk=_tile(n, 512))
        return _output(x, oa, ob, gates, bm, wpa, wpb, wo, _tile(n, 512))

    return (encode(x_prompt), encode(x_sample))
```

```python
import functools

import jax
import jax.numpy as jnp
from jax import lax
from jax.experimental import pallas as pl
from jax.experimental.pallas import tpu as pltpu

F32 = jnp.float32
BF16 = jnp.bfloat16

D_MODEL = 1024
N_META = 16
GRID_W = 64
ROPE_THETA = 10000.0
EPS = 1e-6

HEADS_A = 8
KV_HEADS_A = 2
HEAD_DIM_A = 128
GROUP_A = HEADS_A // KV_HEADS_A

HEADS_B = 8
Q_LORA = 384
KV_LORA = 256
NOPE_B = 128
ROPE_B = 64
V_B = 128
QK_B = NOPE_B + ROPE_B

LANES = 128
QK_B_PAD = 2 * LANES
META_PAD = LANES
VMEM_LIMIT_BYTES = 56 * 1024 * 1024

C_QA = 0
C_KA = C_QA + HEADS_A * HEAD_DIM_A
C_VA = C_KA + KV_HEADS_A * HEAD_DIM_A
C_CQ = C_VA + KV_HEADS_A * HEAD_DIM_A
C_CKV = C_CQ + Q_LORA
C_KR = C_CKV + KV_LORA
C_GATES = C_KR + LANES
N_GATES = 4 * D_MODEL
C_END = C_GATES + N_GATES

NEG_BIG = -1e30
LOG2_E = 1.4426950408889634
MAX_UNSHIFTED_LOGIT = 60.0


def _rms_scale(x, dim):
    return lax.rsqrt(jnp.sum(x * x, axis=-1, keepdims=True) * (1.0 / dim) + EPS)


def _rope(y, tab_ref, shift):
    up = pltpu.roll(y, LANES - shift, 1)
    down = pltpu.roll(y, shift, 1)
    return y * tab_ref[0] + up * tab_ref[1] + down * tab_ref[2]


def _proj_kernel(x_ref, gain_ref, w_in_ref, w_uq_ref, w_ukv_ref, qna_ref, kna_ref, cqn_ref, ckvn_ref,
                 qnb_ref, knb_ref, rope_a_ref, rope_b_ref,
                 qa_ref, ka_ref, va_ref, qb_ref, kb_ref, vb_ref, gates_ref):
    x = x_ref[0]
    u = (x * _rms_scale(x, D_MODEL) * gain_ref[...]).astype(BF16)

    def proj(c0, c1):
        return jnp.dot(u, w_in_ref[:, c0:c1], preferred_element_type=F32)

    gates_ref[0] = proj(C_GATES, C_END)

    zq = proj(C_QA, C_KA)
    gq = qna_ref[...] * (HEAD_DIM_A ** -0.5 * LOG2_E)
    for h in range(HEADS_A):
        xh = zq[:, h * LANES:(h + 1) * LANES]
        y = xh * _rms_scale(xh, HEAD_DIM_A) * gq
        qa_ref[0, h] = _rope(y, rope_a_ref, HEAD_DIM_A // 4).astype(BF16)
    zk = proj(C_KA, C_VA)
    for h in range(KV_HEADS_A):
        xh = zk[:, h * LANES:(h + 1) * LANES]
        y = xh * _rms_scale(xh, HEAD_DIM_A) * kna_ref[...]
        ka_ref[0, h] = _rope(y, rope_a_ref, HEAD_DIM_A // 4).astype(BF16)
    zv = proj(C_VA, C_CQ)
    for h in range(KV_HEADS_A):
        va_ref[0, h] = zv[:, h * LANES:(h + 1) * LANES].astype(BF16)

    cq = proj(C_CQ, C_CKV)
    cqn = (cq * _rms_scale(cq, Q_LORA) * cqn_ref[...]).astype(BF16)
    zqb = jnp.dot(cqn, w_uq_ref[...], preferred_element_type=F32)
    gqb = qnb_ref[...] * (QK_B ** -0.5 * LOG2_E)
    for h in range(HEADS_B):
        xh = zqb[:, h * QK_B_PAD:(h + 1) * QK_B_PAD]
        y = xh * _rms_scale(xh, QK_B) * gqb
        qb_ref[0, h, :, :LANES] = y[:, :LANES].astype(BF16)
        qb_ref[0, h, :, LANES:] = _rope(y[:, LANES:], rope_b_ref, ROPE_B // 4).astype(BF16)

    ckv = proj(C_CKV, C_KR)
    ckvn = (ckv * _rms_scale(ckv, KV_LORA) * ckvn_ref[...]).astype(BF16)
    zkv = jnp.dot(ckvn, w_ukv_ref[...], preferred_element_type=F32)
    kr = proj(C_KR, C_GATES)
    kr_ss = jnp.sum(kr * kr, axis=-1, keepdims=True)
    gk_nope = knb_ref[:, :LANES]
    gk_rope = knb_ref[:, LANES:]
    for h in range(HEADS_B):
        kn = zkv[:, h * 2 * LANES:h * 2 * LANES + LANES]
        ss = jnp.sum(kn * kn, axis=-1, keepdims=True) + kr_ss
        r = lax.rsqrt(ss * (1.0 / QK_B) + EPS)
        kb_ref[0, h, :, :LANES] = (kn * r * gk_nope).astype(BF16)
        kb_ref[0, h, :, LANES:] = _rope(kr * r * gk_rope, rope_b_ref, ROPE_B // 4).astype(BF16)
        vb_ref[0, h] = zkv[:, h * 2 * LANES + LANES:(h + 1) * 2 * LANES].astype(BF16)


def _const_spec(shape):
    return pl.BlockSpec(shape, lambda *_: (0,) * len(shape))


def _project(x, tm, gain, w_in, w_uq, w_ukv, qna, kna, cqn, ckvn, qnb, knb, rope_a, rope_b):
    b, n, _ = x.shape
    grid = (b, n // tm)

    def head_out(heads, width):
        return (jax.ShapeDtypeStruct((b, heads, n, width), BF16),
                pl.BlockSpec((1, heads, tm, width), lambda bi, i: (bi, 0, i, 0)))

    outs = [head_out(HEADS_A, HEAD_DIM_A), head_out(KV_HEADS_A, HEAD_DIM_A), head_out(KV_HEADS_A, HEAD_DIM_A),
            head_out(HEADS_B, QK_B_PAD), head_out(HEADS_B, QK_B_PAD), head_out(HEADS_B, V_B),
            (jax.ShapeDtypeStruct((b, n, N_GATES), F32), pl.BlockSpec((1, tm, N_GATES), lambda bi, i: (bi, i, 0)))]
    consts = (gain, w_in, w_uq, w_ukv, qna, kna, cqn, ckvn, qnb, knb)
    rope_spec = pl.BlockSpec((3, tm, LANES), lambda bi, i: (0, i, 0))
    return pl.pallas_call(
        _proj_kernel,
        grid=grid,
        in_specs=[pl.BlockSpec((1, tm, D_MODEL), lambda bi, i: (bi, i, 0))]
        + [_const_spec(c.shape) for c in consts] + [rope_spec, rope_spec],
        out_specs=[o[1] for o in outs],
        out_shape=[o[0] for o in outs],
        compiler_params=pltpu.CompilerParams(
            dimension_semantics=("arbitrary", "arbitrary"), vmem_limit_bytes=VMEM_LIMIT_BYTES),
        name="in_proj",
    )(x, *consts, rope_a, rope_b)


def _lane_block_sum(p):
    out = p[:, :LANES]
    for c in range(1, p.shape[1] // LANES):
        out = out + p[:, c * LANES:(c + 1) * LANES]
    return out


def _attn_kernel(safe_ref, q_ref, k_ref, v_ref, km_ref, vm_ref, g_ref, o_ref, m_sc, l_sc, acc_sc,
                 *, group, tq, tk, tk_online):
    m_rows = group * tq
    n = k_ref.shape[2]
    q = q_ref[0].reshape(m_rows, q_ref.shape[-1])
    nt = (((1,), (1,)), ((), ()))
    s_meta = lax.dot_general(q, km_ref[0], nt, preferred_element_type=F32)
    is_meta = lax.broadcasted_iota(jnp.int32, s_meta.shape, 1) < N_META

    def kv_tile(j, width):
        off = pl.multiple_of(j * width, width)
        return k_ref[0, 0, pl.ds(off, width), :], v_ref[0, 0, pl.ds(off, width), :]

    @pl.when(safe_ref[0] != 0)
    def _():
        p = jnp.where(is_meta, jnp.exp2(s_meta), 0.0)
        l_sc[...] = p
        acc_sc[...] = jnp.dot(p.astype(BF16), vm_ref[0], preferred_element_type=F32)

        def body(j, carry):
            k, v = kv_tile(j, tk)
            p = jnp.exp2(lax.dot_general(q, k, nt, preferred_element_type=F32))
            l_sc[...] += _lane_block_sum(p)
            acc_sc[...] += jnp.dot(p.astype(BF16), v, preferred_element_type=F32)
            return carry

        lax.fori_loop(0, n // tk, body, 0)
        l_sc[...] = jnp.broadcast_to(jnp.sum(l_sc[...], axis=1, keepdims=True), l_sc.shape)

    @pl.when(safe_ref[0] == 0)
    def _():
        s = jnp.where(is_meta, s_meta, NEG_BIG)
        m0 = jnp.max(s, axis=1, keepdims=True)
        p = jnp.exp2(s - m0)
        m_sc[...] = jnp.broadcast_to(m0, m_sc.shape)
        l_sc[...] = jnp.broadcast_to(jnp.sum(p, axis=1, keepdims=True), l_sc.shape)
        acc_sc[...] = jnp.dot(p.astype(BF16), vm_ref[0], preferred_element_type=F32)

        def body(j, carry):
            k, v = kv_tile(j, tk_online)
            s = lax.dot_general(q, k, nt, preferred_element_type=F32)
            m_prev = m_sc[...]
            m_new = jnp.maximum(m_prev, jnp.max(s, axis=1, keepdims=True))
            alpha = jnp.exp2(m_prev - m_new)
            p = jnp.exp2(s - jnp.concatenate([m_new] * (tk_online // LANES), axis=1))
            l_sc[...] = alpha * l_sc[...] + jnp.sum(p, axis=1, keepdims=True)
            acc_sc[...] = alpha * acc_sc[...] + jnp.dot(p.astype(BF16), v, preferred_element_type=F32)
            m_sc[...] = m_new
            return carry

        lax.fori_loop(0, n // tk_online, body, 0)

    o = acc_sc[...] / l_sc[...]
    for h in range(group):
        g = g_ref[0, :, h * LANES:(h + 1) * LANES]
        o_ref[0, :, h * LANES:(h + 1) * LANES] = (o[h * tq:(h + 1) * tq] * (g * jax.nn.sigmoid(g))).astype(BF16)


def _attention(safe, q, k, v, k_meta, v_meta, gates, *, group, gate_col, tq, tk, tk_online):
    b, hq, n, dk = q.shape
    hkv = k.shape[1]
    dv = v.shape[-1]
    width = group * dv
    gate_blk = gate_col // width
    kernel = functools.partial(_attn_kernel, group=group, tq=tq, tk=tk, tk_online=tk_online)
    grid_spec = pltpu.PrefetchScalarGridSpec(
        num_scalar_prefetch=1,
        grid=(b, hkv, n // tq),
        in_specs=[
            pl.BlockSpec((1, group, tq, dk), lambda bi, g, i, s: (bi, g, i, 0)),
            pl.BlockSpec((1, 1, n, dk), lambda bi, g, i, s: (bi, g, 0, 0)),
            pl.BlockSpec((1, 1, n, dv), lambda bi, g, i, s: (bi, g, 0, 0)),
            pl.BlockSpec((1, META_PAD, dk), lambda bi, g, i, s: (g, 0, 0)),
            pl.BlockSpec((1, META_PAD, dv), lambda bi, g, i, s: (g, 0, 0)),
            pl.BlockSpec((1, tq, width), lambda bi, g, i, s: (bi, i, gate_blk + g)),
        ],
        out_specs=pl.BlockSpec((1, tq, width), lambda bi, g, i, s: (bi, i, g)),
        scratch_shapes=[pltpu.VMEM((group * tq, LANES), F32)] * 2 + [pltpu.VMEM((group * tq, dv), F32)],
    )
    return pl.pallas_call(
        kernel,
        grid_spec=grid_spec,
        out_shape=jax.ShapeDtypeStruct((b, n, hq * dv), BF16),
        compiler_params=pltpu.CompilerParams(
            dimension_semantics=("arbitrary", "arbitrary", "arbitrary"), vmem_limit_bytes=VMEM_LIMIT_BYTES),
        name="attn_g%d" % group,
    )(safe, q, k, v, k_meta, v_meta, gates)


def _logits_bounded(g_q, g_k, head_dim):
    bound = jnp.max(jnp.abs(g_q)) * jnp.max(jnp.abs(g_k)) * (head_dim ** 0.5)
    return (bound <= MAX_UNSHIFTED_LOGIT).astype(jnp.int32).reshape(1)


def _out_kernel(x_ref, oa_ref, ob_ref, ma_ref, mb_ref, bm_ref, wpa_ref, wpb_ref, wo_ref, y_ref):
    ya = jnp.dot(oa_ref[0], wpa_ref[...], preferred_element_type=F32)
    yb = jnp.dot(ob_ref[0], wpb_ref[...], preferred_element_type=F32)
    mix = jax.nn.sigmoid(ma_ref[0] + bm_ref[0:1]) * ya + jax.nn.sigmoid(mb_ref[0] + bm_ref[1:2]) * yb
    y_ref[0] = x_ref[0] + jnp.dot(mix.astype(BF16), wo_ref[...], preferred_element_type=F32)


def _output(x, oa, ob, gates, b_merge, w_proj_a, w_proj_b, w_out, tm):
    b, n, _ = x.shape
    tok = lambda c: pl.BlockSpec((1, tm, D_MODEL), lambda bi, i: (bi, i, c))
    return pl.pallas_call(
        _out_kernel,
        grid=(b, n // tm),
        in_specs=[tok(0), tok(0), tok(0), tok(2), tok(3), _const_spec(b_merge.shape),
                  _const_spec(w_proj_a.shape), _const_spec(w_proj_b.shape), _const_spec(w_out.shape)],
        out_specs=tok(0),
        out_shape=jax.ShapeDtypeStruct(x.shape, F32),
        compiler_params=pltpu.CompilerParams(
            dimension_semantics=("arbitrary", "arbitrary"), vmem_limit_bytes=VMEM_LIMIT_BYTES),
        name="out_proj",
    )(x, oa, ob, gates, gates, b_merge, w_proj_a, w_proj_b, w_out)


def _rope_tables(n, rot_dim):
    rows = n // GRID_W
    axis_dim = rot_dim // 2
    inv_freq = ROPE_THETA ** (-jnp.arange(0, axis_dim, 2, dtype=F32) / axis_dim)
    row = jnp.repeat(jnp.arange(rows, dtype=F32), GRID_W)
    col = jnp.tile(jnp.arange(GRID_W, dtype=F32), rows)
    ang_r = row[:, None] * inv_freq[None, :]
    ang_c = col[:, None] * inv_freq[None, :]
    cr, sr, cc, sc = jnp.cos(ang_r), jnp.sin(ang_r), jnp.cos(ang_c), jnp.sin(ang_c)
    z = jnp.zeros_like(cr)
    tabs = jnp.stack([jnp.concatenate([cr, cr, cc, cc], axis=1),
                      jnp.concatenate([-sr, z, -sc, z], axis=1),
                      jnp.concatenate([z, sr, z, sc], axis=1)])
    return jnp.pad(tabs, ((0, 0), (0, 0), (0, LANES - rot_dim)))


def _identity_rope(n, rot_dim):
    cos = jnp.pad(jnp.ones((n, rot_dim), F32), ((0, 0), (0, LANES - rot_dim)))
    return jnp.stack([cos, jnp.zeros_like(cos), jnp.zeros_like(cos)])


def _pad_meta(a):
    return jnp.pad(a[0], ((0, 0), (0, META_PAD - N_META), (0, 0)))


def _tile(n, pref):
    return pref if n % pref == 0 else n


def kernel(x_prompt, x_sample, meta_tokens, norm_gain, w_in, b_merge, q_norm_a, k_norm_a, cq_norm, ckv_norm,
           w_uq, w_ukv, q_norm_b, k_norm_b, w_proj_a, w_proj_b, w_out):
    assert norm_gain.shape[0] == 1, "meta-token queries are only skipped for a single layer"
    wi = w_in[0]
    s0 = HEADS_A * HEAD_DIM_A
    s1 = s0 + 2 * KV_HEADS_A * HEAD_DIM_A
    s2 = s1 + HEADS_A * HEAD_DIM_A
    s3 = s2 + Q_LORA + KV_LORA
    s4 = s3 + ROPE_B
    w_in_p = jnp.concatenate(
        [wi[:, :s1], wi[:, s2:s4], jnp.zeros((D_MODEL, LANES - ROPE_B), wi.dtype), wi[:, s1:s2], wi[:, s4:]],
        axis=1).astype(BF16)
    w_uq_p = jnp.pad(w_uq[0].reshape(Q_LORA, HEADS_B, QK_B), ((0, 0), (0, 0), (0, QK_B_PAD - QK_B))
                     ).reshape(Q_LORA, HEADS_B * QK_B_PAD).astype(BF16)
    w_ukv_b = w_ukv[0].astype(BF16)
    row = lambda a: a.reshape(1, -1).astype(F32)
    pad_b = lambda a: jnp.pad(row(a), ((0, 0), (0, QK_B_PAD - QK_B)))
    norms = (row(q_norm_a[0]), row(k_norm_a[0]), row(cq_norm[0]), row(ckv_norm[0]),
             pad_b(q_norm_b[0]), pad_b(k_norm_b[0]))
    gain = row(norm_gain[0])
    proj_w = (gain, w_in_p, w_uq_p, w_ukv_b) + norms
    wpa, wpb, wo = w_proj_a[0].astype(BF16), w_proj_b[0].astype(BF16), w_out[0].astype(BF16)
    bm = b_merge[0].astype(F32)

    meta = _project(meta_tokens[None].astype(F32), N_META, *proj_w,
                    _identity_rope(N_META, HEAD_DIM_A), _identity_rope(N_META, ROPE_B))
    ka_m, va_m, kb_m, vb_m = (_pad_meta(meta[i]) for i in (1, 2, 4, 5))
    safe_a = _logits_bounded(q_norm_a[0], k_norm_a[0], HEAD_DIM_A)
    safe_b = _logits_bounded(q_norm_b[0], k_norm_b[0], QK_B)

    def encode(x):
        n = x.shape[1]
        qa, ka, va, qb, kb, vb, gates = _project(
            x, _tile(n, 256), *proj_w, _rope_tables(n, HEAD_DIM_A), _rope_tables(n, ROPE_B))
        oa = _attention(safe_a, qa, ka, va, ka_m, va_m, gates, group=GROUP_A, gate_col=0,
                        tq=_tile(n, 128), tk=_tile(n, 2048), tk_online=_tile(n, 512))
        ob = _attention(safe_b, qb, kb, vb, kb_m, vb_m, gates, group=1, gate_col=D_MODEL,
                        tq=_tile(n, 512), tk=_tile(n, 2048), tk_online=_tile(n, 512))
        return _output(x, oa, ob, gates, bm, wpa, wpb, wo, _tile(n, 512))

    return (encode(x_prompt), encode(x_sample))
```

```python
import functools

import jax
import jax.numpy as jnp
from jax import lax
from jax.experimental import pallas as pl
from jax.experimental.pallas import tpu as pltpu

F32 = jnp.float32
BF16 = jnp.bfloat16

D_MODEL = 1024
N_META = 16
GRID_W = 64
ROPE_THETA = 10000.0
EPS = 1e-6

HEADS_A = 8
KV_HEADS_A = 2
HEAD_DIM_A = 128
GROUP_A = HEADS_A // KV_HEADS_A

HEADS_B = 8
Q_LORA = 384
KV_LORA = 256
NOPE_B = 128
ROPE_B = 64
V_B = 128
QK_B = NOPE_B + ROPE_B

LANES = 128
QK_B_PAD = 2 * LANES
META_PAD = LANES
VMEM_LIMIT_BYTES = 56 * 1024 * 1024

C_QA = 0
C_KA = C_QA + HEADS_A * HEAD_DIM_A
C_VA = C_KA + KV_HEADS_A * HEAD_DIM_A
C_CQ = C_VA + KV_HEADS_A * HEAD_DIM_A
C_CKV = C_CQ + Q_LORA
C_KR = C_CKV + KV_LORA
C_GATES = C_KR + LANES
N_GATES = 4 * D_MODEL
C_END = C_GATES + N_GATES

NEG_BIG = -1e30
LOG2_E = 1.4426950408889634
MAX_UNSHIFTED_LOGIT = 60.0


def _rms_scale(x, dim):
    return lax.rsqrt(jnp.sum(x * x, axis=-1, keepdims=True) * (1.0 / dim) + EPS)


def _rope(y, tab_ref, shift):
    up = pltpu.roll(y, LANES - shift, 1)
    down = pltpu.roll(y, shift, 1)
    return y * tab_ref[0] + up * tab_ref[1] + down * tab_ref[2]


def _proj_kernel(x_ref, gain_ref, w_in_ref, w_uq_ref, w_ukv_ref, qna_ref, kna_ref, cqn_ref, ckvn_ref,
                 qnb_ref, knb_ref, rope_a_ref, rope_b_ref,
                 qa_ref, ka_ref, va_ref, qb_ref, kb_ref, vb_ref, gates_ref):
    x = x_ref[0]
    u = (x * _rms_scale(x, D_MODEL) * gain_ref[...]).astype(BF16)

    def proj(c0, c1):
        return jnp.dot(u, w_in_ref[:, c0:c1], preferred_element_type=F32)

    zq = proj(C_QA, C_KA)
    gq = qna_ref[...] * (HEAD_DIM_A ** -0.5 * LOG2_E)
    for h in range(HEADS_A):
        xh = zq[:, h * LANES:(h + 1) * LANES]
        y = xh * _rms_scale(xh, HEAD_DIM_A) * gq
        qa_ref[0, h] = _rope(y, rope_a_ref, HEAD_DIM_A // 4).astype(BF16)
    zk = proj(C_KA, C_VA)
    for h in range(KV_HEADS_A):
        xh = zk[:, h * LANES:(h + 1) * LANES]
        y = xh * _rms_scale(xh, HEAD_DIM_A) * kna_ref[...]
        ka_ref[0, h] = _rope(y, rope_a_ref, HEAD_DIM_A // 4).astype(BF16)
    zv = proj(C_VA, C_CQ)
    for h in range(KV_HEADS_A):
        va_ref[0, h] = zv[:, h * LANES:(h + 1) * LANES].astype(BF16)

    cq = proj(C_CQ, C_CKV)
    cqn = (cq * _rms_scale(cq, Q_LORA) * cqn_ref[...]).astype(BF16)
    zqb = jnp.dot(cqn, w_uq_ref[...], preferred_element_type=F32)
    gqb = qnb_ref[...] * (QK_B ** -0.5 * LOG2_E)
    for h in range(HEADS_B):
        xh = zqb[:, h * QK_B_PAD:(h + 1) * QK_B_PAD]
        y = xh * _rms_scale(xh, QK_B) * gqb
        qb_ref[0, h, :, :LANES] = y[:, :LANES].astype(BF16)
        qb_ref[0, h, :, LANES:] = _rope(y[:, LANES:], rope_b_ref, ROPE_B // 4).astype(BF16)

    ckv = proj(C_CKV, C_KR)
    ckvn = (ckv * _rms_scale(ckv, KV_LORA) * ckvn_ref[...]).astype(BF16)
    zkv = jnp.dot(ckvn, w_ukv_ref[...], preferred_element_type=F32)
    kr = proj(C_KR, C_GATES)
    kr_ss = jnp.sum(kr * kr, axis=-1, keepdims=True)
    gk_nope = knb_ref[:, :LANES]
    gk_rope = knb_ref[:, LANES:]
    for h in range(HEADS_B):
        kn = zkv[:, h * 2 * LANES:h * 2 * LANES + LANES]
        ss = jnp.sum(kn * kn, axis=-1, keepdims=True) + kr_ss
        r = lax.rsqrt(ss * (1.0 / QK_B) + EPS)
        kb_ref[0, h, :, :LANES] = (kn * r * gk_nope).astype(BF16)
        kb_ref[0, h, :, LANES:] = _rope(kr * r * gk_rope, rope_b_ref, ROPE_B // 4).astype(BF16)
        vb_ref[0, h] = zkv[:, h * 2 * LANES + LANES:(h + 1) * 2 * LANES].astype(BF16)

    gates_ref[0] = proj(C_GATES, C_END)


def _const_spec(shape):
    return pl.BlockSpec(shape, lambda *_: (0,) * len(shape))


def _project(x, tm, gain, w_in, w_uq, w_ukv, qna, kna, cqn, ckvn, qnb, knb, rope_a, rope_b):
    b, n, _ = x.shape
    grid = (b, n // tm)

    def head_out(heads, width):
        return (jax.ShapeDtypeStruct((b, heads, n, width), BF16),
                pl.BlockSpec((1, heads, tm, width), lambda bi, i: (bi, 0, i, 0)))

    outs = [head_out(HEADS_A, HEAD_DIM_A), head_out(KV_HEADS_A, HEAD_DIM_A), head_out(KV_HEADS_A, HEAD_DIM_A),
            head_out(HEADS_B, QK_B_PAD), head_out(HEADS_B, QK_B_PAD), head_out(HEADS_B, V_B),
            (jax.ShapeDtypeStruct((b, n, N_GATES), F32), pl.BlockSpec((1, tm, N_GATES), lambda bi, i: (bi, i, 0)))]
    consts = (gain, w_in, w_uq, w_ukv, qna, kna, cqn, ckvn, qnb, knb)
    rope_spec = pl.BlockSpec((3, tm, LANES), lambda bi, i: (0, i, 0))
    return pl.pallas_call(
        _proj_kernel,
        grid=grid,
        in_specs=[pl.BlockSpec((1, tm, D_MODEL), lambda bi, i: (bi, i, 0))]
        + [_const_spec(c.shape) for c in consts] + [rope_spec, rope_spec],
        out_specs=[o[1] for o in outs],
        out_shape=[o[0] for o in outs],
        compiler_params=pltpu.CompilerParams(
            dimension_semantics=("arbitrary", "arbitrary"), vmem_limit_bytes=VMEM_LIMIT_BYTES),
        name="in_proj",
    )(x, *consts, rope_a, rope_b)


def _lane_block_sum(p):
    out = p[:, :LANES]
    for c in range(1, p.shape[1] // LANES):
        out = out + p[:, c * LANES:(c + 1) * LANES]
    return out


def _attn_kernel(safe_ref, q_ref, k_ref, v_ref, km_ref, vm_ref, g_ref, o_ref, m_sc, l_sc, acc_sc,
                 *, group, tq, n_sub, tk, tk_online):
    tqs = tq // n_sub
    ms = group * tqs
    n = k_ref.shape[2]
    nt = (((1,), (1,)), ((), ()))
    subs = range(n_sub)
    rows = [pl.ds(s * ms, ms) for s in subs]

    def q_sub(s):
        return q_ref[0, :, s * tqs:(s + 1) * tqs, :].reshape(ms, q_ref.shape[-1])

    is_meta = lax.broadcasted_iota(jnp.int32, (ms, META_PAD), 1) < N_META

    def meta_logits(s):
        return lax.dot_general(q_sub(s), km_ref[0], nt, preferred_element_type=F32)

    def kv_tile(j, width):
        off = pl.multiple_of(j * width, width)
        return k_ref[0, 0, pl.ds(off, width), :], v_ref[0, 0, pl.ds(off, width), :]

    @pl.when(safe_ref[0] != 0)
    def _():
        for s in subs:
            p = jnp.where(is_meta, jnp.exp2(meta_logits(s)), 0.0)
            l_sc[rows[s], :] = p
            acc_sc[rows[s], :] = jnp.dot(p.astype(BF16), vm_ref[0], preferred_element_type=F32)

        def body(j, carry):
            k, v = kv_tile(j, tk)
            for s in subs:
                p = jnp.exp2(lax.dot_general(q_sub(s), k, nt, preferred_element_type=F32))
                l_sc[rows[s], :] += _lane_block_sum(p)
                acc_sc[rows[s], :] += jnp.dot(p.astype(BF16), v, preferred_element_type=F32)
            return carry

        lax.fori_loop(0, n // tk, body, 0)
        l_sc[...] = jnp.broadcast_to(jnp.sum(l_sc[...], axis=1, keepdims=True), l_sc.shape)

    @pl.when(safe_ref[0] == 0)
    def _():
        for s in subs:
            sm = jnp.where(is_meta, meta_logits(s), NEG_BIG)
            m0 = jnp.max(sm, axis=1, keepdims=True)
            p = jnp.exp2(sm - m0)
            m_sc[rows[s], :] = jnp.broadcast_to(m0, (ms, LANES))
            l_sc[rows[s], :] = jnp.broadcast_to(jnp.sum(p, axis=1, keepdims=True), (ms, LANES))
            acc_sc[rows[s], :] = jnp.dot(p.astype(BF16), vm_ref[0], preferred_element_type=F32)

        def body(j, carry):
            k, v = kv_tile(j, tk_online)
            for s in subs:
                sc = lax.dot_general(q_sub(s), k, nt, preferred_element_type=F32)
                m_prev = m_sc[rows[s], :]
                m_new = jnp.maximum(m_prev, jnp.max(sc, axis=1, keepdims=True))
                alpha = jnp.exp2(m_prev - m_new)
                p = jnp.exp2(sc - jnp.concatenate([m_new] * (tk_online // LANES), axis=1))
                l_sc[rows[s], :] = alpha * l_sc[rows[s], :] + jnp.sum(p, axis=1, keepdims=True)
                acc_sc[rows[s], :] = alpha * acc_sc[rows[s], :] + jnp.dot(
                    p.astype(BF16), v, preferred_element_type=F32)
                m_sc[rows[s], :] = m_new
            return carry

        lax.fori_loop(0, n // tk_online, body, 0)

    for s in subs:
        o = acc_sc[rows[s], :] / l_sc[rows[s], :]
        for h in range(group):
            g = g_ref[0, s * tqs:(s + 1) * tqs, h * LANES:(h + 1) * LANES]
            o_ref[0, s * tqs:(s + 1) * tqs, h * LANES:(h + 1) * LANES] = (
                o[h * tqs:(h + 1) * tqs] * (g * jax.nn.sigmoid(g))).astype(BF16)


def _attention(safe, q, k, v, k_meta, v_meta, gates, *, group, gate_col, tq, n_sub, tk, tk_online):
    b, hq, n, dk = q.shape
    hkv = k.shape[1]
    dv = v.shape[-1]
    width = group * dv
    gate_blk = gate_col // width
    kernel = functools.partial(_attn_kernel, group=group, tq=tq, n_sub=n_sub, tk=tk, tk_online=tk_online)
    grid_spec = pltpu.PrefetchScalarGridSpec(
        num_scalar_prefetch=1,
        grid=(b, hkv, n // tq),
        in_specs=[
            pl.BlockSpec((1, group, tq, dk), lambda bi, g, i, s: (bi, g, i, 0)),
            pl.BlockSpec((1, 1, n, dk), lambda bi, g, i, s: (bi, g, 0, 0)),
            pl.BlockSpec((1, 1, n, dv), lambda bi, g, i, s: (bi, g, 0, 0)),
            pl.BlockSpec((1, META_PAD, dk), lambda bi, g, i, s: (g, 0, 0)),
            pl.BlockSpec((1, META_PAD, dv), lambda bi, g, i, s: (g, 0, 0)),
            pl.BlockSpec((1, tq, width), lambda bi, g, i, s: (bi, i, gate_blk + g)),
        ],
        out_specs=pl.BlockSpec((1, tq, width), lambda bi, g, i, s: (bi, i, g)),
        scratch_shapes=[pltpu.VMEM((group * tq, LANES), F32)] * 2 + [pltpu.VMEM((group * tq, dv), F32)],
    )
    return pl.pallas_call(
        kernel,
        grid_spec=grid_spec,
        out_shape=jax.ShapeDtypeStruct((b, n, hq * dv), BF16),
        compiler_params=pltpu.CompilerParams(
            dimension_semantics=("arbitrary", "arbitrary", "arbitrary"), vmem_limit_bytes=VMEM_LIMIT_BYTES),
        name="attn_g%d" % group,
    )(safe, q, k, v, k_meta, v_meta, gates)


def _logits_bounded(g_q, g_k, head_dim):
    bound = jnp.max(jnp.abs(g_q)) * jnp.max(jnp.abs(g_k)) * (head_dim ** 0.5)
    return (bound <= MAX_UNSHIFTED_LOGIT).astype(jnp.int32).reshape(1)


def _out_kernel(x_ref, oa_ref, ob_ref, ma_ref, mb_ref, bm_ref, wpa_ref, wpb_ref, wo_ref, y_ref):
    ya = jnp.dot(oa_ref[0], wpa_ref[...], preferred_element_type=F32)
    yb = jnp.dot(ob_ref[0], wpb_ref[...], preferred_element_type=F32)
    mix = jax.nn.sigmoid(ma_ref[0] + bm_ref[0:1]) * ya + jax.nn.sigmoid(mb_ref[0] + bm_ref[1:2]) * yb
    y_ref[0] = x_ref[0] + jnp.dot(mix.astype(BF16), wo_ref[...], preferred_element_type=F32)


def _output(x, oa, ob, gates, b_merge, w_proj_a, w_proj_b, w_out, tm):
    b, n, _ = x.shape
    tok = lambda c: pl.BlockSpec((1, tm, D_MODEL), lambda bi, i: (bi, i, c))
    return pl.pallas_call(
        _out_kernel,
        grid=(b, n // tm),
        in_specs=[tok(0), tok(0), tok(0), tok(2), tok(3), _const_spec(b_merge.shape),
                  _const_spec(w_proj_a.shape), _const_spec(w_proj_b.shape), _const_spec(w_out.shape)],
        out_specs=tok(0),
        out_shape=jax.ShapeDtypeStruct(x.shape, F32),
        compiler_params=pltpu.CompilerParams(
            dimension_semantics=("arbitrary", "arbitrary"), vmem_limit_bytes=VMEM_LIMIT_BYTES),
        name="out_proj",
    )(x, oa, ob, gates, gates, b_merge, w_proj_a, w_proj_b, w_out)


def _rope_tables(n, rot_dim):
    rows = n // GRID_W
    axis_dim = rot_dim // 2
    inv_freq = ROPE_THETA ** (-jnp.arange(0, axis_dim, 2, dtype=F32) / axis_dim)
    ang_r = jnp.arange(rows, dtype=F32)[:, None] * inv_freq[None, :]
    ang_c = jnp.arange(GRID_W, dtype=F32)[:, None] * inv_freq[None, :]
    cr, sr = (jnp.repeat(f(ang_r), GRID_W, axis=0) for f in (jnp.cos, jnp.sin))
    cc, sc = (jnp.tile(f(ang_c), (rows, 1)) for f in (jnp.cos, jnp.sin))
    z = jnp.zeros_like(cr)
    tabs = jnp.stack([jnp.concatenate([cr, cr, cc, cc], axis=1),
                      jnp.concatenate([-sr, z, -sc, z], axis=1),
                      jnp.concatenate([z, sr, z, sc], axis=1)])
    return jnp.pad(tabs, ((0, 0), (0, 0), (0, LANES - rot_dim)))


def _identity_rope(n, rot_dim):
    cos = jnp.pad(jnp.ones((n, rot_dim), F32), ((0, 0), (0, LANES - rot_dim)))
    return jnp.stack([cos, jnp.zeros_like(cos), jnp.zeros_like(cos)])


def _pad_meta(a):
    return jnp.pad(a[0], ((0, 0), (0, META_PAD - N_META), (0, 0)))


def _tile(n, pref):
    return pref if n % pref == 0 else n


def kernel(x_prompt, x_sample, meta_tokens, norm_gain, w_in, b_merge, q_norm_a, k_norm_a, cq_norm, ckv_norm,
           w_uq, w_ukv, q_norm_b, k_norm_b, w_proj_a, w_proj_b, w_out):
    assert norm_gain.shape[0] == 1, "meta-token queries are only skipped for a single layer"
    wi = w_in[0]
    s0 = HEADS_A * HEAD_DIM_A
    s1 = s0 + 2 * KV_HEADS_A * HEAD_DIM_A
    s2 = s1 + HEADS_A * HEAD_DIM_A
    s3 = s2 + Q_LORA + KV_LORA
    s4 = s3 + ROPE_B
    w_in_p = jnp.concatenate(
        [wi[:, :s1], wi[:, s2:s4], jnp.zeros((D_MODEL, LANES - ROPE_B), wi.dtype), wi[:, s1:s2], wi[:, s4:]],
        axis=1).astype(BF16)
    w_uq_p = jnp.pad(w_uq[0].reshape(Q_LORA, HEADS_B, QK_B), ((0, 0), (0, 0), (0, QK_B_PAD - QK_B))
                     ).reshape(Q_LORA, HEADS_B * QK_B_PAD).astype(BF16)
    w_ukv_b = w_ukv[0].astype(BF16)
    row = lambda a: a.reshape(1, -1).astype(F32)
    pad_b = lambda a: jnp.pad(row(a), ((0, 0), (0, QK_B_PAD - QK_B)))
    norms = (row(q_norm_a[0]), row(k_norm_a[0]), row(cq_norm[0]), row(ckv_norm[0]),
             pad_b(q_norm_b[0]), pad_b(k_norm_b[0]))
    gain = row(norm_gain[0])
    proj_w = (gain, w_in_p, w_uq_p, w_ukv_b) + norms
    wpa, wpb, wo = w_proj_a[0].astype(BF16), w_proj_b[0].astype(BF16), w_out[0].astype(BF16)
    bm = b_merge[0].astype(F32)

    meta = _project(meta_tokens[None].astype(F32), N_META, *proj_w,
                    _identity_rope(N_META, HEAD_DIM_A), _identity_rope(N_META, ROPE_B))
    ka_m, va_m, kb_m, vb_m = (_pad_meta(meta[i]) for i in (1, 2, 4, 5))
    safe_a = _logits_bounded(q_norm_a[0], k_norm_a[0], HEAD_DIM_A)
    safe_b = _logits_bounded(q_norm_b[0], k_norm_b[0], QK_B)

    n_max = max(x_prompt.shape[1], x_sample.shape[1])
    rope_a, rope_b = _rope_tables(n_max, HEAD_DIM_A), _rope_tables(n_max, ROPE_B)

    def encode(x):
        n = x.shape[1]
        qa, ka, va, qb, kb, vb, gates = _project(x, _tile(n, 256), *proj_w, rope_a, rope_b)
        oa = _attention(safe_a, qa, ka, va, ka_m, va_m, gates, group=GROUP_A, gate_col=0,
                        tq=_tile(n, 512), n_sub=4, tk=_tile(n, 2048), tk_online=_tile(n, 512))
        ob = _attention(safe_b, qb, kb, vb, kb_m, vb_m, gates, group=1, gate_col=D_MODEL,
                        tq=_tile(n, 2048), n_sub=4, tk=_tile(n, 2048), tk_online=_tile(n, 512))
        return _output(x, oa, ob, gates, bm, wpa, wpb, wo, _tile(n, 512))

    return (encode(x_prompt), encode(x_sample))
```

```python
import functools

import numpy as np
import jax
import jax.numpy as jnp
from jax import lax
from jax.experimental import pallas as pl
from jax.experimental.pallas import tpu as pltpu

F32 = jnp.float32
BF16 = jnp.bfloat16

D_MODEL = 1024
N_META = 16
GRID_W = 64
ROPE_THETA = 10000.0
EPS = 1e-6

HEADS_A = 8
KV_HEADS_A = 2
HEAD_DIM_A = 128
GROUP_A = HEADS_A // KV_HEADS_A

HEADS_B = 8
Q_LORA = 384
KV_LORA = 256
NOPE_B = 128
ROPE_B = 64
V_B = 128
QK_B = NOPE_B + ROPE_B

LANES = 128
HALF = LANES // 2
QK_B_PAD = 2 * LANES
META_PAD = LANES
VMEM_LIMIT_BYTES = 56 * 1024 * 1024

C_QA = 0
C_KA = C_QA + HEADS_A * HEAD_DIM_A
C_VA = C_KA + KV_HEADS_A * HEAD_DIM_A
C_CQ = C_VA + KV_HEADS_A * HEAD_DIM_A
C_CKV = C_CQ + Q_LORA
C_KR = C_CKV + KV_LORA
C_QKV_END = C_KR + LANES
N_GATES = 4 * D_MODEL

NEG_BIG = -1e30
LOG2_E = 1.4426950408889634
MAX_UNSHIFTED_LOGIT = 60.0


def _rotary_lanes(rot_dim):
    q = rot_dim // 4
    first = np.concatenate([np.arange(q), 2 * q + np.arange(q)])
    pad = np.full(HALF - 2 * q, -1)
    return np.concatenate([first, pad, first + q, pad])


LANES_ROPE_A = _rotary_lanes(HEAD_DIM_A)
LANES_ROPE_B = _rotary_lanes(ROPE_B)


def _rms_scale(x, dim):
    return lax.rsqrt(jnp.sum(x * x, axis=-1, keepdims=True) * (1.0 / dim) + EPS)


def _rope(y, tab_ref, rs):
    return y * tab_ref[0, rs, :] + pltpu.roll(y, HALF, 1) * tab_ref[1, rs, :]


def _normed_input(x):
    return x * _rms_scale(x, D_MODEL)


def _qkv_kernel(x_ref, gain_ref, w_in_ref, w_uq_ref, w_ukv_ref, qna_ref, kna_ref, cqn_ref, ckvn_ref,
                qnb_ref, knb_ref, rope_a_ref, rope_b_ref,
                qa_ref, ka_ref, va_ref, qb_ref, kb_ref, vb_ref, *, n_sub):
    tms = x_ref.shape[1] // n_sub
    gq = qna_ref[...] * (HEAD_DIM_A ** -0.5 * LOG2_E)
    gqb = qnb_ref[...] * (QK_B ** -0.5 * LOG2_E)
    gk_nope = knb_ref[:, :LANES]
    gk_rope = knb_ref[:, LANES:]

    for t in range(n_sub):
        rs = pl.ds(t * tms, tms)
        u = (_normed_input(x_ref[0, rs, :]) * gain_ref[...]).astype(BF16)

        def proj(c0, c1):
            return jnp.dot(u, w_in_ref[:, c0:c1], preferred_element_type=F32)

        zq = proj(C_QA, C_KA)
        for h in range(HEADS_A):
            xh = zq[:, h * LANES:(h + 1) * LANES]
            y = xh * _rms_scale(xh, HEAD_DIM_A) * gq
            qa_ref[0, h, rs, :] = _rope(y, rope_a_ref, rs).astype(BF16)
        zk = proj(C_KA, C_VA)
        for h in range(KV_HEADS_A):
            xh = zk[:, h * LANES:(h + 1) * LANES]
            y = xh * _rms_scale(xh, HEAD_DIM_A) * kna_ref[...]
            ka_ref[0, h, rs, :] = _rope(y, rope_a_ref, rs).astype(BF16)
        zv = proj(C_VA, C_CQ)
        for h in range(KV_HEADS_A):
            va_ref[0, h, rs, :] = zv[:, h * LANES:(h + 1) * LANES].astype(BF16)

        cq = proj(C_CQ, C_CKV)
        cqn = (cq * _rms_scale(cq, Q_LORA) * cqn_ref[...]).astype(BF16)
        zqb = jnp.dot(cqn, w_uq_ref[...], preferred_element_type=F32)
        for h in range(HEADS_B):
            xh = zqb[:, h * QK_B_PAD:(h + 1) * QK_B_PAD]
            y = xh * _rms_scale(xh, QK_B) * gqb
            qb_ref[0, h, rs, :LANES] = y[:, :LANES].astype(BF16)
            qb_ref[0, h, rs, LANES:] = _rope(y[:, LANES:], rope_b_ref, rs).astype(BF16)

        ckv = proj(C_CKV, C_KR)
        ckvn = (ckv * _rms_scale(ckv, KV_LORA) * ckvn_ref[...]).astype(BF16)
        zkv = jnp.dot(ckvn, w_ukv_ref[...], preferred_element_type=F32)
        kr = proj(C_KR, C_QKV_END)
        kr_ss = jnp.sum(kr * kr, axis=-1, keepdims=True)
        kr_rot = _rope(kr * gk_rope, rope_b_ref, rs)
        for h in range(HEADS_B):
            kn = zkv[:, h * 2 * LANES:h * 2 * LANES + LANES]
            ss = jnp.sum(kn * kn, axis=-1, keepdims=True) + kr_ss
            r = lax.rsqrt(ss * (1.0 / QK_B) + EPS)
            kb_ref[0, h, rs, :LANES] = (kn * r * gk_nope).astype(BF16)
            kb_ref[0, h, rs, LANES:] = (kr_rot * r).astype(BF16)
            vb_ref[0, h, rs, :] = zkv[:, h * 2 * LANES + LANES:(h + 1) * 2 * LANES].astype(BF16)


def _const_spec(shape):
    return pl.BlockSpec(shape, lambda *_: (0,) * len(shape))


def _project_qkv(x, tm, n_sub, gain, w_qkv, w_uq, w_ukv, qna, kna, cqn, ckvn, qnb, knb, rope_a, rope_b):
    b, n, _ = x.shape

    def head_out(heads, width):
        return (jax.ShapeDtypeStruct((b, heads, n, width), BF16),
                pl.BlockSpec((1, heads, tm, width), lambda bi, i: (bi, 0, i, 0)))

    outs = [head_out(HEADS_A, HEAD_DIM_A), head_out(KV_HEADS_A, HEAD_DIM_A), head_out(KV_HEADS_A, HEAD_DIM_A),
            head_out(HEADS_B, QK_B_PAD), head_out(HEADS_B, QK_B_PAD), head_out(HEADS_B, V_B)]
    consts = (gain, w_qkv, w_uq, w_ukv, qna, kna, cqn, ckvn, qnb, knb)
    rope_spec = pl.BlockSpec((2, tm, LANES), lambda bi, i: (0, i, 0))
    return pl.pallas_call(
        functools.partial(_qkv_kernel, n_sub=n_sub),
        grid=(b, n // tm),
        in_specs=[pl.BlockSpec((1, tm, D_MODEL), lambda bi, i: (bi, i, 0))]
        + [_const_spec(c.shape) for c in consts] + [rope_spec, rope_spec],
        out_specs=[o[1] for o in outs],
        out_shape=[o[0] for o in outs],
        compiler_params=pltpu.CompilerParams(
            dimension_semantics=("arbitrary", "arbitrary"), vmem_limit_bytes=VMEM_LIMIT_BYTES),
        name="qkv_proj",
    )(x, *consts, rope_a, rope_b)


def _gates_kernel(x_ref, gain_ref, w_ref, o_ref):
    u = (_normed_input(x_ref[0]) * gain_ref[...]).astype(BF16)
    o_ref[0] = jnp.dot(u, w_ref[...], preferred_element_type=F32)


def _project_gates(x, tm, gain, w_gates):
    b, n, _ = x.shape
    return pl.pallas_call(
        _gates_kernel,
        grid=(b, n // tm),
        in_specs=[pl.BlockSpec((1, tm, D_MODEL), lambda bi, i: (bi, i, 0)),
                  _const_spec(gain.shape), _const_spec(w_gates.shape)],
        out_specs=pl.BlockSpec((1, tm, N_GATES), lambda bi, i: (bi, i, 0)),
        out_shape=jax.ShapeDtypeStruct((b, n, N_GATES), F32),
        compiler_params=pltpu.CompilerParams(
            dimension_semantics=("arbitrary", "arbitrary"), vmem_limit_bytes=VMEM_LIMIT_BYTES),
        name="gate_proj",
    )(x, gain, w_gates)


def _lane_block_sum(p):
    out = p[:, :LANES]
    for c in range(1, p.shape[1] // LANES):
        out = out + p[:, c * LANES:(c + 1) * LANES]
    return out


def _attn_kernel(safe_ref, q_ref, k_ref, v_ref, km_ref, vm_ref, g_ref, o_ref, m_sc, l_sc, acc_sc,
                 *, group, tq, n_sub, tk, tk_online):
    tqs = tq // n_sub
    ms = group * tqs
    n = k_ref.shape[2]
    nt = (((1,), (1,)), ((), ()))
    subs = range(n_sub)
    rows = [pl.ds(s * ms, ms) for s in subs]
    is_meta = lax.broadcasted_iota(jnp.int32, (ms, META_PAD), 1) < N_META

    def q_sub(s):
        return q_ref[0, :, s * tqs:(s + 1) * tqs, :].reshape(ms, q_ref.shape[-1])

    def meta_logits(s):
        return lax.dot_general(q_sub(s), km_ref[0], nt, preferred_element_type=F32)

    def kv_tile(j, width):
        off = pl.multiple_of(j * width, width)
        return k_ref[0, 0, pl.ds(off, width), :], v_ref[0, 0, pl.ds(off, width), :]

    def finish(s, l):
        o = acc_sc[rows[s], :] / l
        for h in range(group):
            g = g_ref[0, s * tqs:(s + 1) * tqs, h * LANES:(h + 1) * LANES]
            o_ref[0, s * tqs:(s + 1) * tqs, h * LANES:(h + 1) * LANES] = (
                o[h * tqs:(h + 1) * tqs] * (g * jax.nn.sigmoid(g))).astype(BF16)

    @pl.when(safe_ref[0] != 0)
    def _():
        for s in subs:
            p = jnp.where(is_meta, jnp.exp2(meta_logits(s)), 0.0)
            l_sc[rows[s], :] = p
            acc_sc[rows[s], :] = jnp.dot(p.astype(BF16), vm_ref[0], preferred_element_type=F32)

        def body(j, carry):
            k, v = kv_tile(j, tk)
            for s in subs:
                p = jnp.exp2(lax.dot_general(q_sub(s), k, nt, preferred_element_type=F32))
                l_sc[rows[s], :] += _lane_block_sum(p)
                acc_sc[rows[s], :] += jnp.dot(p.astype(BF16), v, preferred_element_type=F32)
            return carry

        lax.fori_loop(0, n // tk, body, 0)
        for s in subs:
            finish(s, jnp.sum(l_sc[rows[s], :], axis=1, keepdims=True))

    @pl.when(safe_ref[0] == 0)
    def _():
        for s in subs:
            sm = jnp.where(is_meta, meta_logits(s), NEG_BIG)
            m0 = jnp.max(sm, axis=1, keepdims=True)
            p = jnp.exp2(sm - m0)
            m_sc[rows[s], :] = jnp.broadcast_to(m0, (ms, LANES))
            l_sc[rows[s], :] = jnp.broadcast_to(jnp.sum(p, axis=1, keepdims=True), (ms, LANES))
            acc_sc[rows[s], :] = jnp.dot(p.astype(BF16), vm_ref[0], preferred_element_type=F32)

        def body(j, carry):
            k, v = kv_tile(j, tk_online)
            for s in subs:
                sc = lax.dot_general(q_sub(s), k, nt, preferred_element_type=F32)
                m_prev = m_sc[rows[s], :]
                m_new = jnp.maximum(m_prev, jnp.max(sc, axis=1, keepdims=True))
                alpha = jnp.exp2(m_prev - m_new)
                p = jnp.exp2(sc - jnp.concatenate([m_new] * (tk_online // LANES), axis=1))
                l_sc[rows[s], :] = alpha * l_sc[rows[s], :] + jnp.sum(p, axis=1, keepdims=True)
                acc_sc[rows[s], :] = alpha * acc_sc[rows[s], :] + jnp.dot(
                    p.astype(BF16), v, preferred_element_type=F32)
                m_sc[rows[s], :] = m_new
            return carry

        lax.fori_loop(0, n // tk_online, body, 0)
        for s in subs:
            finish(s, l_sc[rows[s], :])


def _attention(safe, q, k, v, k_meta, v_meta, gates, *, group, gate_col, tq, n_sub, tk, tk_online):
    b, hq, n, dk = q.shape
    hkv = k.shape[1]
    dv = v.shape[-1]
    width = group * dv
    gate_blk = gate_col // width
    kernel = functools.partial(_attn_kernel, group=group, tq=tq, n_sub=n_sub, tk=tk, tk_online=tk_online)
    grid_spec = pltpu.PrefetchScalarGridSpec(
        num_scalar_prefetch=1,
        grid=(b, hkv, n // tq),
        in_specs=[
            pl.BlockSpec((1, group, tq, dk), lambda bi, g, i, s: (bi, g, i, 0)),
            pl.BlockSpec((1, 1, n, dk), lambda bi, g, i, s: (bi, g, 0, 0)),
            pl.BlockSpec((1, 1, n, dv), lambda bi, g, i, s: (bi, g, 0, 0)),
            pl.BlockSpec((1, META_PAD, dk), lambda bi, g, i, s: (g, 0, 0)),
            pl.BlockSpec((1, META_PAD, dv), lambda bi, g, i, s: (g, 0, 0)),
            pl.BlockSpec((1, tq, width), lambda bi, g, i, s: (bi, i, gate_blk + g)),
        ],
        out_specs=pl.BlockSpec((1, tq, width), lambda bi, g, i, s: (bi, i, g)),
        scratch_shapes=[pltpu.VMEM((group * tq, LANES), F32)] * 2 + [pltpu.VMEM((group * tq, dv), F32)],
    )
    return pl.pallas_call(
        kernel,
        grid_spec=grid_spec,
        out_shape=jax.ShapeDtypeStruct((b, n, hq * dv), BF16),
        compiler_params=pltpu.CompilerParams(
            dimension_semantics=("arbitrary", "arbitrary", "arbitrary"), vmem_limit_bytes=VMEM_LIMIT_BYTES),
        name="attn_g%d" % group,
    )(safe, q, k, v, k_meta, v_meta, gates)


def _logits_bounded(g_q, g_k, head_dim):
    bound = jnp.max(jnp.abs(g_q)) * jnp.max(jnp.abs(g_k)) * (head_dim ** 0.5)
    return (bound <= MAX_UNSHIFTED_LOGIT).astype(jnp.int32).reshape(1)


def _out_kernel(x_ref, oa_ref, ob_ref, ma_ref, mb_ref, bm_ref, wpa_ref, wpb_ref, wo_ref, y_ref):
    ya = jnp.dot(oa_ref[0], wpa_ref[...], preferred_element_type=F32)
    yb = jnp.dot(ob_ref[0], wpb_ref[...], preferred_element_type=F32)
    mix = jax.nn.sigmoid(ma_ref[0] + bm_ref[0:1]) * ya + jax.nn.sigmoid(mb_ref[0] + bm_ref[1:2]) * yb
    y_ref[0] = x_ref[0] + jnp.dot(mix.astype(BF16), wo_ref[...], preferred_element_type=F32)


def _output(x, oa, ob, gates, b_merge, w_proj_a, w_proj_b, w_out, tm):
    b, n, _ = x.shape
    tok = lambda c: pl.BlockSpec((1, tm, D_MODEL), lambda bi, i: (bi, i, c))
    return pl.pallas_call(
        _out_kernel,
        grid=(b, n // tm),
        in_specs=[tok(0), tok(0), tok(0), tok(2), tok(3), _const_spec(b_merge.shape),
                  _const_spec(w_proj_a.shape), _const_spec(w_proj_b.shape), _const_spec(w_out.shape)],
        out_specs=tok(0),
        out_shape=jax.ShapeDtypeStruct(x.shape, F32),
        compiler_params=pltpu.CompilerParams(
            dimension_semantics=("arbitrary", "arbitrary"), vmem_limit_bytes=VMEM_LIMIT_BYTES),
        name="out_proj",
    )(x, oa, ob, gates, gates, b_merge, w_proj_a, w_proj_b, w_out)


def _rope_tables(n, rot_dim):
    rows = n // GRID_W
    axis_dim = rot_dim // 2
    inv_freq = ROPE_THETA ** (-jnp.arange(0, axis_dim, 2, dtype=F32) / axis_dim)
    ang_r = jnp.arange(rows, dtype=F32)[:, None] * inv_freq[None, :]
    ang_c = jnp.arange(GRID_W, dtype=F32)[:, None] * inv_freq[None, :]
    cr, sr = (jnp.repeat(f(ang_r), GRID_W, axis=0) for f in (jnp.cos, jnp.sin))
    cc, sc = (jnp.tile(f(ang_c), (rows, 1)) for f in (jnp.cos, jnp.sin))
    pad = jnp.zeros((n, HALF - rot_dim // 2), F32)
    return jnp.stack([jnp.concatenate([cr, cc, pad, cr, cc, pad], axis=1),
                      jnp.concatenate([-sr, -sc, pad, sr, sc, pad], axis=1)])


def _identity_rope(n):
    return jnp.stack([jnp.ones((n, LANES), F32), jnp.zeros((n, LANES), F32)])


def _gather_cols(w, idx):
    pieces, i = [], 0
    while i < len(idx):
        j = i + 1
        if idx[i] < 0:
            while j < len(idx) and idx[j] < 0:
                j += 1
            pieces.append(jnp.zeros((w.shape[0], j - i), w.dtype))
        else:
            while j < len(idx) and idx[j] == idx[j - 1] + 1:
                j += 1
            pieces.append(w[:, int(idx[i]):int(idx[j - 1]) + 1])
        i = j
    return jnp.concatenate(pieces, axis=1)


def _pad_meta(a):
    return jnp.pad(a[0], ((0, 0), (0, META_PAD - N_META), (0, 0)))


def _tile(n, pref):
    return pref if n % pref == 0 else n


def kernel(x_prompt, x_sample, meta_tokens, norm_gain, w_in, b_merge, q_norm_a, k_norm_a, cq_norm, ckv_norm,
           w_uq, w_ukv, q_norm_b, k_norm_b, w_proj_a, w_proj_b, w_out):
    assert norm_gain.shape[0] == 1, "meta-token queries are only skipped for a single layer"
    wi = w_in[0]
    s0 = HEADS_A * HEAD_DIM_A
    s1 = s0 + 2 * KV_HEADS_A * HEAD_DIM_A
    s2 = s1 + HEADS_A * HEAD_DIM_A
    s3 = s2 + Q_LORA + KV_LORA
    s4 = s3 + ROPE_B
    rope_b_cols = np.where(LANES_ROPE_B < 0, -1, NOPE_B + LANES_ROPE_B)
    qk_a_cols = np.concatenate([h * HEAD_DIM_A + LANES_ROPE_A for h in range(HEADS_A + KV_HEADS_A)])
    qkv_cols = np.concatenate([qk_a_cols, np.arange(s0 + KV_HEADS_A * HEAD_DIM_A, s1), np.arange(s2, s3),
                               np.where(LANES_ROPE_B < 0, -1, s3 + LANES_ROPE_B)])
    w_qkv = _gather_cols(wi, qkv_cols).astype(BF16)
    w_gates = jnp.concatenate([wi[:, s1:s2], wi[:, s4:]], axis=1).astype(BF16)
    head_b_cols = np.concatenate([np.arange(NOPE_B), rope_b_cols])
    uq_cols = np.concatenate([np.where(head_b_cols < 0, -1, h * QK_B + head_b_cols) for h in range(HEADS_B)])
    w_uq_p = _gather_cols(w_uq[0], uq_cols).astype(BF16)
    w_ukv_b = w_ukv[0].astype(BF16)
    row = lambda a: a.reshape(1, -1).astype(F32)
    norms = (_gather_cols(row(q_norm_a[0]), LANES_ROPE_A), _gather_cols(row(k_norm_a[0]), LANES_ROPE_A),
             row(cq_norm[0]), row(ckv_norm[0]),
             _gather_cols(row(q_norm_b[0]), head_b_cols), _gather_cols(row(k_norm_b[0]), head_b_cols))
    gain = row(norm_gain[0])
    proj_w = (gain, w_qkv, w_uq_p, w_ukv_b) + norms
    wpa, wpb, wo = w_proj_a[0].astype(BF16), w_proj_b[0].astype(BF16), w_out[0].astype(BF16)
    bm = b_merge[0].astype(F32)

    meta = _project_qkv(meta_tokens[None].astype(F32), N_META, 1, *proj_w,
                        _identity_rope(N_META), _identity_rope(N_META))
    ka_m, va_m, kb_m, vb_m = (_pad_meta(meta[i]) for i in (1, 2, 4, 5))
    safe_a = _logits_bounded(q_norm_a[0], k_norm_a[0], HEAD_DIM_A)
    safe_b = _logits_bounded(q_norm_b[0], k_norm_b[0], QK_B)

    n_max = max(x_prompt.shape[1], x_sample.shape[1])
    rope_a, rope_b = _rope_tables(n_max, HEAD_DIM_A), _rope_tables(n_max, ROPE_B)

    def encode(x):
        n = x.shape[1]
        qa, ka, va, qb, kb, vb = _project_qkv(x, _tile(n, 512), 2, *proj_w, rope_a, rope_b)
        gates = _project_gates(x, _tile(n, 512), gain, w_gates)
        oa = _attention(safe_a, qa, ka, va, ka_m, va_m, gates, group=GROUP_A, gate_col=0,
                        tq=_tile(n, 512), n_sub=4, tk=_tile(n, 2048), tk_online=_tile(n, 512))
        ob = _attention(safe_b, qb, kb, vb, kb_m, vb_m, gates, group=1, gate_col=D_MODEL,
                        tq=_tile(n, 2048), n_sub=4, tk=_tile(n, 2048), tk_online=_tile(n, 512))
        return _output(x, oa, ob, gates, bm, wpa, wpb, wo, _tile(n, 512))

    return (encode(x_prompt), encode(x_sample))
```

```python
import functools

import numpy as np
import jax
import jax.numpy as jnp
from jax import lax
from jax.experimental import pallas as pl
from jax.experimental.pallas import tpu as pltpu

F32 = jnp.float32
BF16 = jnp.bfloat16

D_MODEL = 1024
N_META = 16
GRID_W = 64
ROPE_THETA = 10000.0
EPS = 1e-6

HEADS_A = 8
KV_HEADS_A = 2
HEAD_DIM_A = 128
GROUP_A = HEADS_A // KV_HEADS_A

HEADS_B = 8
Q_LORA = 384
KV_LORA = 256
NOPE_B = 128
ROPE_B = 64
V_B = 128
QK_B = NOPE_B + ROPE_B

LANES = 128
HALF = LANES // 2
QK_B_PAD = 2 * LANES
META_PAD = LANES
VMEM_LIMIT_BYTES = 56 * 1024 * 1024

C_QA = 0
C_KA = C_QA + HEADS_A * HEAD_DIM_A
C_VA = C_KA + KV_HEADS_A * HEAD_DIM_A
C_CQ = C_VA + KV_HEADS_A * HEAD_DIM_A
C_CKV = C_CQ + Q_LORA
C_KR = C_CKV + KV_LORA
C_QKV_END = C_KR + LANES
N_GATES = 4 * D_MODEL

NEG_BIG = -1e30
LOG2_E = 1.4426950408889634
MAX_UNSHIFTED_LOGIT = 60.0
KV_UNROLL = 2


def _rotary_lanes(rot_dim):
    q = rot_dim // 4
    first = np.concatenate([np.arange(q), 2 * q + np.arange(q)])
    pad = np.full(HALF - 2 * q, -1)
    return np.concatenate([first, pad, first + q, pad])


LANES_ROPE_A = _rotary_lanes(HEAD_DIM_A)
LANES_ROPE_B = _rotary_lanes(ROPE_B)


def _rms_scale(x, dim):
    return lax.rsqrt(jnp.sum(x * x, axis=-1, keepdims=True) * (1.0 / dim) + EPS)


def _rope(y, tab_ref, rs):
    return y * tab_ref[0, rs, :] + pltpu.roll(y, HALF, 1) * tab_ref[1, rs, :]


def _normed_input(x):
    return x * _rms_scale(x, D_MODEL)


def _qkv_kernel(x_ref, gain_ref, w_in_ref, w_uq_ref, w_ukv_ref, qna_ref, kna_ref, cqn_ref, ckvn_ref,
                qnb_ref, knb_ref, rope_a_ref, rope_b_ref,
                qa_ref, ka_ref, va_ref, qb_ref, kb_ref, vb_ref, *, n_sub):
    tms = x_ref.shape[1] // n_sub
    gq = qna_ref[...] * (HEAD_DIM_A ** -0.5 * LOG2_E)
    gqb = qnb_ref[...] * (QK_B ** -0.5 * LOG2_E)
    gk_nope = knb_ref[:, :LANES]
    gk_rope = knb_ref[:, LANES:]

    for t in range(n_sub):
        rs = pl.ds(t * tms, tms)
        u = (_normed_input(x_ref[0, rs, :]) * gain_ref[...]).astype(BF16)

        def proj(c0, c1):
            return jnp.dot(u, w_in_ref[:, c0:c1], preferred_element_type=F32)

        zq = proj(C_QA, C_KA)
        for h in range(HEADS_A):
            xh = zq[:, h * LANES:(h + 1) * LANES]
            y = xh * _rms_scale(xh, HEAD_DIM_A) * gq
            qa_ref[0, h, rs, :] = _rope(y, rope_a_ref, rs).astype(BF16)
        zk = proj(C_KA, C_VA)
        for h in range(KV_HEADS_A):
            xh = zk[:, h * LANES:(h + 1) * LANES]
            y = xh * _rms_scale(xh, HEAD_DIM_A) * kna_ref[...]
            ka_ref[0, h, rs, :] = _rope(y, rope_a_ref, rs).astype(BF16)
        zv = proj(C_VA, C_CQ)
        for h in range(KV_HEADS_A):
            va_ref[0, h, rs, :] = zv[:, h * LANES:(h + 1) * LANES].astype(BF16)

        cq = proj(C_CQ, C_CKV)
        cqn = (cq * _rms_scale(cq, Q_LORA) * cqn_ref[...]).astype(BF16)
        zqb = jnp.dot(cqn, w_uq_ref[...], preferred_element_type=F32)
        for h in range(HEADS_B):
            xh = zqb[:, h * QK_B_PAD:(h + 1) * QK_B_PAD]
            y = xh * _rms_scale(xh, QK_B) * gqb
            qb_ref[0, h, rs, :LANES] = y[:, :LANES].astype(BF16)
            qb_ref[0, h, rs, LANES:] = _rope(y[:, LANES:], rope_b_ref, rs).astype(BF16)

        ckv = proj(C_CKV, C_KR)
        ckvn = (ckv * _rms_scale(ckv, KV_LORA) * ckvn_ref[...]).astype(BF16)
        zkv = jnp.dot(ckvn, w_ukv_ref[...], preferred_element_type=F32)
        kr = proj(C_KR, C_QKV_END)
        kr_ss = jnp.sum(kr * kr, axis=-1, keepdims=True)
        kr_rot = _rope(kr * gk_rope, rope_b_ref, rs)
        for h in range(HEADS_B):
            kn = zkv[:, h * 2 * LANES:h * 2 * LANES + LANES]
            ss = jnp.sum(kn * kn, axis=-1, keepdims=True) + kr_ss
            r = lax.rsqrt(ss * (1.0 / QK_B) + EPS)
            kb_ref[0, h, rs, :LANES] = (kn * r * gk_nope).astype(BF16)
            kb_ref[0, h, rs, LANES:] = (kr_rot * r).astype(BF16)
            vb_ref[0, h, rs, :] = zkv[:, h * 2 * LANES + LANES:(h + 1) * 2 * LANES].astype(BF16)


def _const_spec(shape):
    return pl.BlockSpec(shape, lambda *_: (0,) * len(shape))


def _project_qkv(x, tm, n_sub, gain, w_qkv, w_uq, w_ukv, qna, kna, cqn, ckvn, qnb, knb, rope_a, rope_b):
    b, n, _ = x.shape

    def head_out(heads, width):
        return (jax.ShapeDtypeStruct((b, heads, n, width), BF16),
                pl.BlockSpec((1, heads, tm, width), lambda bi, i: (bi, 0, i, 0)))

    outs = [head_out(HEADS_A, HEAD_DIM_A), head_out(KV_HEADS_A, HEAD_DIM_A), head_out(KV_HEADS_A, HEAD_DIM_A),
            head_out(HEADS_B, QK_B_PAD), head_out(HEADS_B, QK_B_PAD), head_out(HEADS_B, V_B)]
    consts = (gain, w_qkv, w_uq, w_ukv, qna, kna, cqn, ckvn, qnb, knb)
    rope_spec = pl.BlockSpec((2, tm, LANES), lambda bi, i: (0, i, 0))
    return pl.pallas_call(
        functools.partial(_qkv_kernel, n_sub=n_sub),
        grid=(b, n // tm),
        in_specs=[pl.BlockSpec((1, tm, D_MODEL), lambda bi, i: (bi, i, 0))]
        + [_const_spec(c.shape) for c in consts] + [rope_spec, rope_spec],
        out_specs=[o[1] for o in outs],
        out_shape=[o[0] for o in outs],
        compiler_params=pltpu.CompilerParams(
            dimension_semantics=("arbitrary", "arbitrary"), vmem_limit_bytes=VMEM_LIMIT_BYTES),
        name="qkv_proj",
    )(x, *consts, rope_a, rope_b)


def _gates_kernel(x_ref, gain_ref, w_ref, o_ref):
    u = (_normed_input(x_ref[0]) * gain_ref[...]).astype(BF16)
    o_ref[0] = jnp.dot(u, w_ref[...], preferred_element_type=F32)


def _project_gates(x, tm, gain, w_gates):
    b, n, _ = x.shape
    return pl.pallas_call(
        _gates_kernel,
        grid=(b, n // tm),
        in_specs=[pl.BlockSpec((1, tm, D_MODEL), lambda bi, i: (bi, i, 0)),
                  _const_spec(gain.shape), _const_spec(w_gates.shape)],
        out_specs=pl.BlockSpec((1, tm, N_GATES), lambda bi, i: (bi, i, 0)),
        out_shape=jax.ShapeDtypeStruct((b, n, N_GATES), F32),
        compiler_params=pltpu.CompilerParams(
            dimension_semantics=("arbitrary", "arbitrary"), vmem_limit_bytes=VMEM_LIMIT_BYTES),
        name="gate_proj",
    )(x, gain, w_gates)


def _lane_block_sum(p):
    out = p[:, :LANES]
    for c in range(1, p.shape[1] // LANES):
        out = out + p[:, c * LANES:(c + 1) * LANES]
    return out


def _attn_kernel(safe_ref, q_ref, k_ref, v_ref, km_ref, vm_ref, g_ref, o_ref, m_sc, l_sc, acc_sc,
                 *, group, tq, n_sub, tk, tk_online):
    tqs = tq // n_sub
    ms = group * tqs
    n = k_ref.shape[2]
    nt = (((1,), (1,)), ((), ()))
    subs = range(n_sub)
    rows = [pl.ds(s * ms, ms) for s in subs]
    is_meta = lax.broadcasted_iota(jnp.int32, (ms, META_PAD), 1) < N_META

    def q_sub(s):
        return q_ref[0, :, s * tqs:(s + 1) * tqs, :].reshape(ms, q_ref.shape[-1])

    def meta_logits(s):
        return lax.dot_general(q_sub(s), km_ref[0], nt, preferred_element_type=F32)

    def kv_tile(j, width):
        off = pl.multiple_of(j * width, width)
        return k_ref[0, 0, pl.ds(off, width), :], v_ref[0, 0, pl.ds(off, width), :]

    def finish(s, l):
        o = acc_sc[rows[s], :] / l
        for h in range(group):
            g = g_ref[0, s * tqs:(s + 1) * tqs, h * LANES:(h + 1) * LANES]
            o_ref[0, s * tqs:(s + 1) * tqs, h * LANES:(h + 1) * LANES] = (
                o[h * tqs:(h + 1) * tqs] * (g * jax.nn.sigmoid(g))).astype(BF16)

    @pl.when(safe_ref[0] != 0)
    def _():
        for s in subs:
            p = jnp.where(is_meta, jnp.exp2(meta_logits(s)), 0.0)
            l_sc[rows[s], :] = p
            acc_sc[rows[s], :] = jnp.dot(p.astype(BF16), vm_ref[0], preferred_element_type=F32)

        def body(j, carry):
            k, v = kv_tile(j, tk)
            for s in subs:
                p = jnp.exp2(lax.dot_general(q_sub(s), k, nt, preferred_element_type=F32))
                l_sc[rows[s], :] += _lane_block_sum(p)
                acc_sc[rows[s], :] += jnp.dot(p.astype(BF16), v, preferred_element_type=F32)
            return carry

        lax.fori_loop(0, n // tk, body, 0, unroll=min(n // tk, KV_UNROLL))
        for s in subs:
            finish(s, jnp.sum(l_sc[rows[s], :], axis=1, keepdims=True))

    @pl.when(safe_ref[0] == 0)
    def _():
        for s in subs:
            sm = jnp.where(is_meta, meta_logits(s), NEG_BIG)
            m0 = jnp.max(sm, axis=1, keepdims=True)
            p = jnp.exp2(sm - m0)
            m_sc[rows[s], :] = jnp.broadcast_to(m0, (ms, LANES))
            l_sc[rows[s], :] = jnp.broadcast_to(jnp.sum(p, axis=1, keepdims=True), (ms, LANES))
            acc_sc[rows[s], :] = jnp.dot(p.astype(BF16), vm_ref[0], preferred_element_type=F32)

        def body(j, carry):
            k, v = kv_tile(j, tk_online)
            for s in subs:
                sc = lax.dot_general(q_sub(s), k, nt, preferred_element_type=F32)
                m_prev = m_sc[rows[s], :]
                m_new = jnp.maximum(m_prev, jnp.max(sc, axis=1, keepdims=True))
                alpha = jnp.exp2(m_prev - m_new)
                p = jnp.exp2(sc - jnp.concatenate([m_new] * (tk_online // LANES), axis=1))
                l_sc[rows[s], :] = alpha * l_sc[rows[s], :] + jnp.sum(p, axis=1, keepdims=True)
                acc_sc[rows[s], :] = alpha * acc_sc[rows[s], :] + jnp.dot(
                    p.astype(BF16), v, preferred_element_type=F32)
                m_sc[rows[s], :] = m_new
            return carry

        lax.fori_loop(0, n // tk_online, body, 0)
        for s in subs:
            finish(s, l_sc[rows[s], :])


def _attention(safe, q, k, v, k_meta, v_meta, gates, *, group, gate_col, tq, n_sub, tk, tk_online):
    b, hq, n, dk = q.shape
    hkv = k.shape[1]
    dv = v.shape[-1]
    width = group * dv
    gate_blk = gate_col // width
    kernel = functools.partial(_attn_kernel, group=group, tq=tq, n_sub=n_sub, tk=tk, tk_online=tk_online)
    grid_spec = pltpu.PrefetchScalarGridSpec(
        num_scalar_prefetch=1,
        grid=(b, hkv, n // tq),
        in_specs=[
            pl.BlockSpec((1, group, tq, dk), lambda bi, g, i, s: (bi, g, i, 0)),
            pl.BlockSpec((1, 1, n, dk), lambda bi, g, i, s: (bi, g, 0, 0)),
            pl.BlockSpec((1, 1, n, dv), lambda bi, g, i, s: (bi, g, 0, 0)),
            pl.BlockSpec((1, META_PAD, dk), lambda bi, g, i, s: (g, 0, 0)),
            pl.BlockSpec((1, META_PAD, dv), lambda bi, g, i, s: (g, 0, 0)),
            pl.BlockSpec((1, tq, width), lambda bi, g, i, s: (bi, i, gate_blk + g)),
        ],
        out_specs=pl.BlockSpec((1, tq, width), lambda bi, g, i, s: (bi, i, g)),
        scratch_shapes=[pltpu.VMEM((group * tq, LANES), F32)] * 2 + [pltpu.VMEM((group * tq, dv), F32)],
    )
    return pl.pallas_call(
        kernel,
        grid_spec=grid_spec,
        out_shape=jax.ShapeDtypeStruct((b, n, hq * dv), BF16),
        compiler_params=pltpu.CompilerParams(
            dimension_semantics=("arbitrary", "arbitrary", "arbitrary"), vmem_limit_bytes=VMEM_LIMIT_BYTES),
        name="attn_g%d" % group,
    )(safe, q, k, v, k_meta, v_meta, gates)


def _logits_bounded(g_q, g_k, head_dim):
    bound = jnp.max(jnp.abs(g_q)) * jnp.max(jnp.abs(g_k)) * (head_dim ** 0.5)
    return (bound <= MAX_UNSHIFTED_LOGIT).astype(jnp.int32).reshape(1)


def _out_kernel(x_ref, oa_ref, ob_ref, ma_ref, mb_ref, bm_ref, wpa_ref, wpb_ref, wo_ref, y_ref):
    ya = jnp.dot(oa_ref[0], wpa_ref[...], preferred_element_type=F32)
    yb = jnp.dot(ob_ref[0], wpb_ref[...], preferred_element_type=F32)
    mix = jax.nn.sigmoid(ma_ref[0] + bm_ref[0:1]) * ya + jax.nn.sigmoid(mb_ref[0] + bm_ref[1:2]) * yb
    y_ref[0] = x_ref[0] + jnp.dot(mix.astype(BF16), wo_ref[...], preferred_element_type=F32)


def _output(x, oa, ob, gates, b_merge, w_proj_a, w_proj_b, w_out, tm):
    b, n, _ = x.shape
    tok = lambda c: pl.BlockSpec((1, tm, D_MODEL), lambda bi, i: (bi, i, c))
    return pl.pallas_call(
        _out_kernel,
        grid=(b, n // tm),
        in_specs=[tok(0), tok(0), tok(0), tok(2), tok(3), _const_spec(b_merge.shape),
                  _const_spec(w_proj_a.shape), _const_spec(w_proj_b.shape), _const_spec(w_out.shape)],
        out_specs=tok(0),
        out_shape=jax.ShapeDtypeStruct(x.shape, F32),
        compiler_params=pltpu.CompilerParams(
            dimension_semantics=("arbitrary", "arbitrary"), vmem_limit_bytes=VMEM_LIMIT_BYTES),
        name="out_proj",
    )(x, oa, ob, gates, gates, b_merge, w_proj_a, w_proj_b, w_out)


def _rope_tables(n, rot_dim):
    rows = n // GRID_W
    axis_dim = rot_dim // 2
    inv_freq = ROPE_THETA ** (-jnp.arange(0, axis_dim, 2, dtype=F32) / axis_dim)
    ang_r = jnp.arange(rows, dtype=F32)[:, None] * inv_freq[None, :]
    ang_c = jnp.arange(GRID_W, dtype=F32)[:, None] * inv_freq[None, :]
    cr, sr, cc, sc = jnp.cos(ang_r), jnp.sin(ang_r), jnp.cos(ang_c), jnp.sin(ang_c)
    zr, zc = jnp.zeros_like(cr), jnp.zeros_like(cc)
    pad_r, pad_c = (jnp.zeros((m, HALF - rot_dim // 2), F32) for m in (rows, GRID_W))
    by_row = jnp.stack([jnp.concatenate([cr, zr, pad_r, cr, zr, pad_r], axis=1),
                        jnp.concatenate([-sr, zr, pad_r, sr, zr, pad_r], axis=1)])
    by_col = jnp.stack([jnp.concatenate([zc, cc, pad_c, zc, cc, pad_c], axis=1),
                        jnp.concatenate([zc, -sc, pad_c, zc, sc, pad_c], axis=1)])
    return (by_row[:, :, None, :] + by_col[:, None, :, :]).reshape(2, n, LANES)


def _identity_rope(n):
    return jnp.stack([jnp.ones((n, LANES), F32), jnp.zeros((n, LANES), F32)])


def _gather_cols(w, idx):
    pieces, i = [], 0
    while i < len(idx):
        j = i + 1
        if idx[i] < 0:
            while j < len(idx) and idx[j] < 0:
                j += 1
            pieces.append(jnp.zeros((w.shape[0], j - i), w.dtype))
        else:
            while j < len(idx) and idx[j] == idx[j - 1] + 1:
                j += 1
            pieces.append(w[:, int(idx[i]):int(idx[j - 1]) + 1])
        i = j
    return jnp.concatenate(pieces, axis=1)


def _pad_meta(a):
    return jnp.pad(a[0], ((0, 0), (0, META_PAD - N_META), (0, 0)))


def _tile(n, pref):
    return pref if n % pref == 0 else n


def kernel(x_prompt, x_sample, meta_tokens, norm_gain, w_in, b_merge, q_norm_a, k_norm_a, cq_norm, ckv_norm,
           w_uq, w_ukv, q_norm_b, k_norm_b, w_proj_a, w_proj_b, w_out):
    assert norm_gain.shape[0] == 1, "meta-token queries are only skipped for a single layer"
    wi = w_in[0].astype(BF16)
    s0 = HEADS_A * HEAD_DIM_A
    s1 = s0 + 2 * KV_HEADS_A * HEAD_DIM_A
    s2 = s1 + HEADS_A * HEAD_DIM_A
    s3 = s2 + Q_LORA + KV_LORA
    s4 = s3 + ROPE_B
    rope_b_cols = np.where(LANES_ROPE_B < 0, -1, NOPE_B + LANES_ROPE_B)
    qk_a_cols = np.concatenate([h * HEAD_DIM_A + LANES_ROPE_A for h in range(HEADS_A + KV_HEADS_A)])
    qkv_cols = np.concatenate([qk_a_cols, np.arange(s0 + KV_HEADS_A * HEAD_DIM_A, s1), np.arange(s2, s3),
                               np.where(LANES_ROPE_B < 0, -1, s3 + LANES_ROPE_B)])
    w_qkv = _gather_cols(wi, qkv_cols)
    w_gates = jnp.concatenate([wi[:, s1:s2], wi[:, s4:]], axis=1)
    head_b_cols = np.concatenate([np.arange(NOPE_B), rope_b_cols])
    uq_cols = np.concatenate([np.where(head_b_cols < 0, -1, h * QK_B + head_b_cols) for h in range(HEADS_B)])
    w_uq_p = _gather_cols(w_uq[0].astype(BF16), uq_cols)
    w_ukv_b = w_ukv[0].astype(BF16)
    row = lambda a: a.reshape(1, -1).astype(F32)
    norms = (_gather_cols(row(q_norm_a[0]), LANES_ROPE_A), _gather_cols(row(k_norm_a[0]), LANES_ROPE_A),
             row(cq_norm[0]), row(ckv_norm[0]),
             _gather_cols(row(q_norm_b[0]), head_b_cols), _gather_cols(row(k_norm_b[0]), head_b_cols))
    gain = row(norm_gain[0])
    proj_w = (gain, w_qkv, w_uq_p, w_ukv_b) + norms
    wpa, wpb, wo = w_proj_a[0].astype(BF16), w_proj_b[0].astype(BF16), w_out[0].astype(BF16)
    bm = b_merge[0].astype(F32)

    meta = _project_qkv(meta_tokens[None].astype(F32), N_META, 1, *proj_w,
                        _identity_rope(N_META), _identity_rope(N_META))
    ka_m, va_m, kb_m, vb_m = (_pad_meta(meta[i]) for i in (1, 2, 4, 5))
    safe_a = _logits_bounded(q_norm_a[0], k_norm_a[0], HEAD_DIM_A)
    safe_b = _logits_bounded(q_norm_b[0], k_norm_b[0], QK_B)

    n_max = max(x_prompt.shape[1], x_sample.shape[1])
    rope_a, rope_b = _rope_tables(n_max, HEAD_DIM_A), _rope_tables(n_max, ROPE_B)

    def encode(x):
        n = x.shape[1]
        qa, ka, va, qb, kb, vb = _project_qkv(x, _tile(n, 512), 2, *proj_w, rope_a, rope_b)
        gates = _project_gates(x, _tile(n, 512), gain, w_gates)
        oa = _attention(safe_a, qa, ka, va, ka_m, va_m, gates, group=GROUP_A, gate_col=0,
                        tq=_tile(n, 512), n_sub=4, tk=_tile(n, 2048), tk_online=_tile(n, 512))
        ob = _attention(safe_b, qb, kb, vb, kb_m, vb_m, gates, group=1, gate_col=D_MODEL,
                        tq=_tile(n, 2048), n_sub=2, tk=_tile(n, 2048), tk_online=_tile(n, 512))
        return _output(x, oa, ob, gates, bm, wpa, wpb, wo, _tile(n, 512))

    return (encode(x_prompt), encode(x_sample))
```

```python
import functools

import numpy as np
import jax
import jax.numpy as jnp
from jax import lax
from jax.experimental import pallas as pl
from jax.experimental.pallas import tpu as pltpu

F32 = jnp.float32
BF16 = jnp.bfloat16

D_MODEL = 1024
N_META = 16
GRID_W = 64
ROPE_THETA = 10000.0
EPS = 1e-6

HEADS_A = 8
KV_HEADS_A = 2
HEAD_DIM_A = 128
GROUP_A = HEADS_A // KV_HEADS_A

HEADS_B = 8
Q_LORA = 384
KV_LORA = 256
NOPE_B = 128
ROPE_B = 64
V_B = 128
QK_B = NOPE_B + ROPE_B

LANES = 128
HALF = LANES // 2
QK_B_PAD = 2 * LANES
META_PAD = LANES
VMEM_LIMIT_BYTES = 56 * 1024 * 1024

C_QA = 0
C_KA = C_QA + HEADS_A * HEAD_DIM_A
C_VA = C_KA + KV_HEADS_A * HEAD_DIM_A
C_CQ = C_VA + KV_HEADS_A * HEAD_DIM_A
C_CKV = C_CQ + Q_LORA
C_KR = C_CKV + KV_LORA
C_QKV_END = C_KR + LANES
N_GATES = 2 * D_MODEL

NEG_BIG = -1e30
LOG2_E = 1.4426950408889634
MAX_UNSHIFTED_LOGIT = 60.0
KV_UNROLL = 4


def _rotary_lanes(rot_dim):
    q = rot_dim // 4
    first = np.concatenate([np.arange(q), 2 * q + np.arange(q)])
    pad = np.full(HALF - 2 * q, -1)
    return np.concatenate([first, pad, first + q, pad])


LANES_ROPE_A = _rotary_lanes(HEAD_DIM_A)
LANES_ROPE_B = _rotary_lanes(ROPE_B)


def _rms_scale(x, dim):
    return lax.rsqrt(jnp.sum(x * x, axis=-1, keepdims=True) * (1.0 / dim) + EPS)


def _rope(y, tab_ref, rs):
    return y * tab_ref[0, rs, :] + pltpu.roll(y, HALF, 1) * tab_ref[1, rs, :]


def _normed_input(x):
    return x * _rms_scale(x, D_MODEL)


def _qkv_kernel(x_ref, gain_ref, w_in_ref, w_uq_ref, w_ukv_ref, qna_ref, kna_ref, cqn_ref, ckvn_ref,
                qnb_ref, knb_ref, rope_a_ref, rope_b_ref,
                qa_ref, ka_ref, va_ref, qb_ref, kb_ref, vb_ref, *, n_sub):
    tms = x_ref.shape[1] // n_sub
    gq = qna_ref[...] * (HEAD_DIM_A ** -0.5 * LOG2_E)
    gqb = qnb_ref[...] * (QK_B ** -0.5 * LOG2_E)
    gk_nope = knb_ref[:, :LANES]
    gk_rope = knb_ref[:, LANES:]

    for t in range(n_sub):
        rs = pl.ds(t * tms, tms)
        u = (_normed_input(x_ref[0, rs, :]) * gain_ref[...]).astype(BF16)

        def proj(c0, c1):
            return jnp.dot(u, w_in_ref[:, c0:c1], preferred_element_type=F32)

        zq = proj(C_QA, C_KA)
        for h in range(HEADS_A):
            xh = zq[:, h * LANES:(h + 1) * LANES]
            y = xh * _rms_scale(xh, HEAD_DIM_A) * gq
            qa_ref[0, h, rs, :] = _rope(y, rope_a_ref, rs).astype(BF16)
        zk = proj(C_KA, C_VA)
        for h in range(KV_HEADS_A):
            xh = zk[:, h * LANES:(h + 1) * LANES]
            y = xh * _rms_scale(xh, HEAD_DIM_A) * kna_ref[...]
            ka_ref[0, h, rs, :] = _rope(y, rope_a_ref, rs).astype(BF16)
        zv = proj(C_VA, C_CQ)
        for h in range(KV_HEADS_A):
            va_ref[0, h, rs, :] = zv[:, h * LANES:(h + 1) * LANES].astype(BF16)

        cq = proj(C_CQ, C_CKV)
        cqn = (cq * _rms_scale(cq, Q_LORA) * cqn_ref[...]).astype(BF16)
        zqb = jnp.dot(cqn, w_uq_ref[...], preferred_element_type=F32)
        for h in range(HEADS_B):
            xh = zqb[:, h * QK_B_PAD:(h + 1) * QK_B_PAD]
            y = xh * _rms_scale(xh, QK_B) * gqb
            qb_ref[0, h, rs, :LANES] = y[:, :LANES].astype(BF16)
            qb_ref[0, h, rs, LANES:] = _rope(y[:, LANES:], rope_b_ref, rs).astype(BF16)

        ckv = proj(C_CKV, C_KR)
        ckvn = (ckv * _rms_scale(ckv, KV_LORA) * ckvn_ref[...]).astype(BF16)
        zkv = jnp.dot(ckvn, w_ukv_ref[...], preferred_element_type=F32)
        kr = proj(C_KR, C_QKV_END)
        kr_ss = jnp.sum(kr * kr, axis=-1, keepdims=True)
        kr_rot = _rope(kr * gk_rope, rope_b_ref, rs)
        for h in range(HEADS_B):
            kn = zkv[:, h * 2 * LANES:h * 2 * LANES + LANES]
            ss = jnp.sum(kn * kn, axis=-1, keepdims=True) + kr_ss
            r = lax.rsqrt(ss * (1.0 / QK_B) + EPS)
            kb_ref[0, h, rs, :LANES] = (kn * r * gk_nope).astype(BF16)
            kb_ref[0, h, rs, LANES:] = (kr_rot * r).astype(BF16)
            vb_ref[0, h, rs, :] = zkv[:, h * 2 * LANES + LANES:(h + 1) * 2 * LANES].astype(BF16)


def _const_spec(shape):
    return pl.BlockSpec(shape, lambda *_: (0,) * len(shape))


def _project_qkv(x, tm, n_sub, gain, w_qkv, w_uq, w_ukv, qna, kna, cqn, ckvn, qnb, knb, rope_a, rope_b):
    b, n, _ = x.shape

    def head_out(heads, width):
        return (jax.ShapeDtypeStruct((b, heads, n, width), BF16),
                pl.BlockSpec((1, heads, tm, width), lambda bi, i: (bi, 0, i, 0)))

    outs = [head_out(HEADS_A, HEAD_DIM_A), head_out(KV_HEADS_A, HEAD_DIM_A), head_out(KV_HEADS_A, HEAD_DIM_A),
            head_out(HEADS_B, QK_B_PAD), head_out(HEADS_B, QK_B_PAD), head_out(HEADS_B, V_B)]
    consts = (gain, w_qkv, w_uq, w_ukv, qna, kna, cqn, ckvn, qnb, knb)
    rope_spec = pl.BlockSpec((2, tm, LANES), lambda bi, i: (0, i, 0))
    return pl.pallas_call(
        functools.partial(_qkv_kernel, n_sub=n_sub),
        grid=(b, n // tm),
        in_specs=[pl.BlockSpec((1, tm, D_MODEL), lambda bi, i: (bi, i, 0))]
        + [_const_spec(c.shape) for c in consts] + [rope_spec, rope_spec],
        out_specs=[o[1] for o in outs],
        out_shape=[o[0] for o in outs],
        compiler_params=pltpu.CompilerParams(
            dimension_semantics=("arbitrary", "arbitrary"), vmem_limit_bytes=VMEM_LIMIT_BYTES),
        name="qkv_proj",
    )(x, *consts, rope_a, rope_b)


def _gates_kernel(x_ref, gain_ref, w_ref, o_ref):
    u = (_normed_input(x_ref[0]) * gain_ref[...]).astype(BF16)
    o_ref[0] = jnp.dot(u, w_ref[...], preferred_element_type=F32)


def _project_gates(x, tm, gain, w_gates):
    b, n, _ = x.shape
    return pl.pallas_call(
        _gates_kernel,
        grid=(b, n // tm),
        in_specs=[pl.BlockSpec((1, tm, D_MODEL), lambda bi, i: (bi, i, 0)),
                  _const_spec(gain.shape), _const_spec(w_gates.shape)],
        out_specs=pl.BlockSpec((1, tm, N_GATES), lambda bi, i: (bi, i, 0)),
        out_shape=jax.ShapeDtypeStruct((b, n, N_GATES), F32),
        compiler_params=pltpu.CompilerParams(
            dimension_semantics=("arbitrary", "arbitrary"), vmem_limit_bytes=VMEM_LIMIT_BYTES),
        name="gate_proj",
    )(x, gain, w_gates)


def _lane_block_sum(p):
    out = p[:, :LANES]
    for c in range(1, p.shape[1] // LANES):
        out = out + p[:, c * LANES:(c + 1) * LANES]
    return out


def _attn_kernel(safe_ref, q_ref, k_ref, v_ref, km_ref, vm_ref, g_ref, o_ref, m_sc, l_sc, acc_sc,
                 *, group, tq, n_sub, tk, tk_online):
    tqs = tq // n_sub
    ms = group * tqs
    n = k_ref.shape[2]
    nt = (((1,), (1,)), ((), ()))
    subs = range(n_sub)
    rows = [pl.ds(s * ms, ms) for s in subs]
    is_meta = lax.broadcasted_iota(jnp.int32, (ms, META_PAD), 1) < N_META

    def q_sub(s):
        return q_ref[0, :, s * tqs:(s + 1) * tqs, :].reshape(ms, q_ref.shape[-1])

    def meta_logits(s):
        return lax.dot_general(q_sub(s), km_ref[0], nt, preferred_element_type=F32)

    def kv_tile(j, width):
        off = pl.multiple_of(j * width, width)
        return k_ref[0, 0, pl.ds(off, width), :], v_ref[0, 0, pl.ds(off, width), :]

    def finish(s, l):
        o = acc_sc[rows[s], :] / l
        for h in range(group):
            g = g_ref[0, s * tqs:(s + 1) * tqs, h * LANES:(h + 1) * LANES]
            o_ref[0, s * tqs:(s + 1) * tqs, h * LANES:(h + 1) * LANES] = (
                o[h * tqs:(h + 1) * tqs] * (g * jax.nn.sigmoid(g))).astype(BF16)

    @pl.when(safe_ref[0] != 0)
    def _():
        for s in subs:
            p = jnp.where(is_meta, jnp.exp2(meta_logits(s)), 0.0)
            l_sc[rows[s], :] = p
            acc_sc[rows[s], :] = jnp.dot(p.astype(BF16), vm_ref[0], preferred_element_type=F32)

        def body(j, carry):
            k, v = kv_tile(j, tk)
            for s in subs:
                p = jnp.exp2(lax.dot_general(q_sub(s), k, nt, preferred_element_type=F32))
                l_sc[rows[s], :] += _lane_block_sum(p)
                acc_sc[rows[s], :] += jnp.dot(p.astype(BF16), v, preferred_element_type=F32)
            return carry

        lax.fori_loop(0, n // tk, body, 0, unroll=min(n // tk, KV_UNROLL))
        for s in subs:
            finish(s, jnp.sum(l_sc[rows[s], :], axis=1, keepdims=True))

    @pl.when(safe_ref[0] == 0)
    def _():
        for s in subs:
            sm = jnp.where(is_meta, meta_logits(s), NEG_BIG)
            m0 = jnp.max(sm, axis=1, keepdims=True)
            p = jnp.exp2(sm - m0)
            m_sc[rows[s], :] = jnp.broadcast_to(m0, (ms, LANES))
            l_sc[rows[s], :] = jnp.broadcast_to(jnp.sum(p, axis=1, keepdims=True), (ms, LANES))
            acc_sc[rows[s], :] = jnp.dot(p.astype(BF16), vm_ref[0], preferred_element_type=F32)

        def body(j, carry):
            k, v = kv_tile(j, tk_online)
            for s in subs:
                sc = lax.dot_general(q_sub(s), k, nt, preferred_element_type=F32)
                m_prev = m_sc[rows[s], :]
                m_new = jnp.maximum(m_prev, jnp.max(sc, axis=1, keepdims=True))
                alpha = jnp.exp2(m_prev - m_new)
                p = jnp.exp2(sc - jnp.concatenate([m_new] * (tk_online // LANES), axis=1))
                l_sc[rows[s], :] = alpha * l_sc[rows[s], :] + jnp.sum(p, axis=1, keepdims=True)
                acc_sc[rows[s], :] = alpha * acc_sc[rows[s], :] + jnp.dot(
                    p.astype(BF16), v, preferred_element_type=F32)
                m_sc[rows[s], :] = m_new
            return carry

        lax.fori_loop(0, n // tk_online, body, 0)
        for s in subs:
            finish(s, l_sc[rows[s], :])


def _attention(safe, q, k, v, k_meta, v_meta, gates, *, group, gate_col, tq, n_sub, tk, tk_online):
    b, hq, n, dk = q.shape
    hkv = k.shape[1]
    dv = v.shape[-1]
    width = group * dv
    gate_blk = gate_col // width
    kernel = functools.partial(_attn_kernel, group=group, tq=tq, n_sub=n_sub, tk=tk, tk_online=tk_online)
    grid_spec = pltpu.PrefetchScalarGridSpec(
        num_scalar_prefetch=1,
        grid=(b, hkv, n // tq),
        in_specs=[
            pl.BlockSpec((1, group, tq, dk), lambda bi, g, i, s: (bi, g, i, 0)),
            pl.BlockSpec((1, 1, n, dk), lambda bi, g, i, s: (bi, g, 0, 0)),
            pl.BlockSpec((1, 1, n, dv), lambda bi, g, i, s: (bi, g, 0, 0)),
            pl.BlockSpec((1, META_PAD, dk), lambda bi, g, i, s: (g, 0, 0)),
            pl.BlockSpec((1, META_PAD, dv), lambda bi, g, i, s: (g, 0, 0)),
            pl.BlockSpec((1, tq, width), lambda bi, g, i, s: (bi, i, gate_blk + g)),
        ],
        out_specs=pl.BlockSpec((1, tq, width), lambda bi, g, i, s: (bi, i, g)),
        scratch_shapes=[pltpu.VMEM((group * tq, LANES), F32)] * 2 + [pltpu.VMEM((group * tq, dv), F32)],
    )
    return pl.pallas_call(
        kernel,
        grid_spec=grid_spec,
        out_shape=jax.ShapeDtypeStruct((b, n, hq * dv), BF16),
        compiler_params=pltpu.CompilerParams(
            dimension_semantics=("arbitrary", "arbitrary", "arbitrary"), vmem_limit_bytes=VMEM_LIMIT_BYTES),
        name="attn_g%d" % group,
    )(safe, q, k, v, k_meta, v_meta, gates)


def _logits_bounded(g_q, g_k, head_dim):
    bound = jnp.max(jnp.abs(g_q)) * jnp.max(jnp.abs(g_k)) * (head_dim ** 0.5)
    return (bound <= MAX_UNSHIFTED_LOGIT).astype(jnp.int32).reshape(1)


def _out_kernel(x_ref, oa_ref, ob_ref, gain_ref, wm_ref, bm_ref, wpa_ref, wpb_ref, wo_ref, y_ref):
    x = x_ref[0]
    u = (_normed_input(x) * gain_ref[...]).astype(BF16)
    m = jnp.dot(u, wm_ref[...], preferred_element_type=F32)
    ya = jnp.dot(oa_ref[0], wpa_ref[...], preferred_element_type=F32)
    yb = jnp.dot(ob_ref[0], wpb_ref[...], preferred_element_type=F32)
    mix = (jax.nn.sigmoid(m[:, :D_MODEL] + bm_ref[0:1]) * ya
           + jax.nn.sigmoid(m[:, D_MODEL:] + bm_ref[1:2]) * yb)
    y_ref[0] = x + jnp.dot(mix.astype(BF16), wo_ref[...], preferred_element_type=F32)


def _output(x, oa, ob, gain, w_merge, b_merge, w_proj_a, w_proj_b, w_out, tm):
    b, n, _ = x.shape
    tok = pl.BlockSpec((1, tm, D_MODEL), lambda bi, i: (bi, i, 0))
    consts = (gain, w_merge, b_merge, w_proj_a, w_proj_b, w_out)
    return pl.pallas_call(
        _out_kernel,
        grid=(b, n // tm),
        in_specs=[tok, tok, tok] + [_const_spec(c.shape) for c in consts],
        out_specs=tok,
        out_shape=jax.ShapeDtypeStruct(x.shape, F32),
        compiler_params=pltpu.CompilerParams(
            dimension_semantics=("arbitrary", "arbitrary"), vmem_limit_bytes=VMEM_LIMIT_BYTES),
        name="out_proj",
    )(x, oa, ob, *consts)


def _rope_tables(n, rot_dim):
    rows = n // GRID_W
    axis_dim = rot_dim // 2
    inv_freq = ROPE_THETA ** (-jnp.arange(0, axis_dim, 2, dtype=F32) / axis_dim)
    ang_r = jnp.arange(rows, dtype=F32)[:, None] * inv_freq[None, :]
    ang_c = jnp.arange(GRID_W, dtype=F32)[:, None] * inv_freq[None, :]
    cr, sr, cc, sc = jnp.cos(ang_r), jnp.sin(ang_r), jnp.cos(ang_c), jnp.sin(ang_c)
    zr, zc = jnp.zeros_like(cr), jnp.zeros_like(cc)
    pad_r, pad_c = (jnp.zeros((m, HALF - rot_dim // 2), F32) for m in (rows, GRID_W))
    by_row = jnp.stack([jnp.concatenate([cr, zr, pad_r, cr, zr, pad_r], axis=1),
                        jnp.concatenate([-sr, zr, pad_r, sr, zr, pad_r], axis=1)])
    by_col = jnp.stack([jnp.concatenate([zc, cc, pad_c, zc, cc, pad_c], axis=1),
                        jnp.concatenate([zc, -sc, pad_c, zc, sc, pad_c], axis=1)])
    return (by_row[:, :, None, :] + by_col[:, None, :, :]).reshape(2, n, LANES)


def _identity_rope(n):
    return jnp.stack([jnp.ones((n, LANES), F32), jnp.zeros((n, LANES), F32)])


def _gather_cols(w, idx):
    pieces, i = [], 0
    while i < len(idx):
        j = i + 1
        if idx[i] < 0:
            while j < len(idx) and idx[j] < 0:
                j += 1
            pieces.append(jnp.zeros((w.shape[0], j - i), w.dtype))
        else:
            while j < len(idx) and idx[j] == idx[j - 1] + 1:
                j += 1
            pieces.append(w[:, int(idx[i]):int(idx[j - 1]) + 1])
        i = j
    return jnp.concatenate(pieces, axis=1)


def _pad_meta(a):
    return jnp.pad(a[0], ((0, 0), (0, META_PAD - N_META), (0, 0)))


def _tile(n, pref):
    return pref if n % pref == 0 else n


def kernel(x_prompt, x_sample, meta_tokens, norm_gain, w_in, b_merge, q_norm_a, k_norm_a, cq_norm, ckv_norm,
           w_uq, w_ukv, q_norm_b, k_norm_b, w_proj_a, w_proj_b, w_out):
    assert norm_gain.shape[0] == 1, "meta-token queries are only skipped for a single layer"
    wi = w_in[0].astype(BF16)
    s0 = HEADS_A * HEAD_DIM_A
    s1 = s0 + 2 * KV_HEADS_A * HEAD_DIM_A
    s2 = s1 + HEADS_A * HEAD_DIM_A
    s3 = s2 + Q_LORA + KV_LORA
    s4 = s3 + ROPE_B
    rope_b_cols = np.where(LANES_ROPE_B < 0, -1, NOPE_B + LANES_ROPE_B)
    qk_a_cols = np.concatenate([h * HEAD_DIM_A + LANES_ROPE_A for h in range(HEADS_A + KV_HEADS_A)])
    qkv_cols = np.concatenate([qk_a_cols, np.arange(s0 + KV_HEADS_A * HEAD_DIM_A, s1), np.arange(s2, s3),
                               np.where(LANES_ROPE_B < 0, -1, s3 + LANES_ROPE_B)])
    w_qkv = _gather_cols(wi, qkv_cols)
    w_gates = jnp.concatenate([wi[:, s1:s2], wi[:, s4:s4 + D_MODEL]], axis=1)
    w_merge = wi[:, s4 + D_MODEL:]
    head_b_cols = np.concatenate([np.arange(NOPE_B), rope_b_cols])
    uq_cols = np.concatenate([np.where(head_b_cols < 0, -1, h * QK_B + head_b_cols) for h in range(HEADS_B)])
    w_uq_p = _gather_cols(w_uq[0].astype(BF16), uq_cols)
    w_ukv_b = w_ukv[0].astype(BF16)
    row = lambda a: a.reshape(1, -1).astype(F32)
    norms = (_gather_cols(row(q_norm_a[0]), LANES_ROPE_A), _gather_cols(row(k_norm_a[0]), LANES_ROPE_A),
             row(cq_norm[0]), row(ckv_norm[0]),
             _gather_cols(row(q_norm_b[0]), head_b_cols), _gather_cols(row(k_norm_b[0]), head_b_cols))
    gain = row(norm_gain[0])
    proj_w = (gain, w_qkv, w_uq_p, w_ukv_b) + norms
    wpa, wpb, wo = w_proj_a[0].astype(BF16), w_proj_b[0].astype(BF16), w_out[0].astype(BF16)
    bm = b_merge[0].astype(F32)

    meta = _project_qkv(meta_tokens[None].astype(F32), N_META, 1, *proj_w,
                        _identity_rope(N_META), _identity_rope(N_META))
    ka_m, va_m, kb_m, vb_m = (_pad_meta(meta[i]) for i in (1, 2, 4, 5))
    safe_a = _logits_bounded(q_norm_a[0], k_norm_a[0], HEAD_DIM_A)
    safe_b = _logits_bounded(q_norm_b[0], k_norm_b[0], QK_B)

    n_max = max(x_prompt.shape[1], x_sample.shape[1])
    rope_a, rope_b = _rope_tables(n_max, HEAD_DIM_A), _rope_tables(n_max, ROPE_B)

    def encode(x):
        n = x.shape[1]
        qa, ka, va, qb, kb, vb = _project_qkv(x, _tile(n, 512), 2, *proj_w, rope_a, rope_b)
        gates = _project_gates(x, _tile(n, 1024), gain, w_gates)
        oa = _attention(safe_a, qa, ka, va, ka_m, va_m, gates, group=GROUP_A, gate_col=0,
                        tq=_tile(n, 512), n_sub=4, tk=_tile(n, 2048), tk_online=_tile(n, 512))
        ob = _attention(safe_b, qb, kb, vb, kb_m, vb_m, gates, group=1, gate_col=D_MODEL,
                        tq=_tile(n, 2048), n_sub=2, tk=_tile(n, 2048), tk_online=_tile(n, 512))
        return _output(x, oa, ob, gain, w_merge, bm, wpa, wpb, wo, _tile(n, 512))

    return (encode(x_prompt), encode(x_sample))
```

```python
import functools

import numpy as np
import jax
import jax.numpy as jnp
from jax import lax
from jax.experimental import pallas as pl
from jax.experimental.pallas import tpu as pltpu

F32 = jnp.float32
BF16 = jnp.bfloat16

D_MODEL = 1024
N_META = 16
GRID_W = 64
ROPE_THETA = 10000.0
EPS = 1e-6

HEADS_A = 8
KV_HEADS_A = 2
HEAD_DIM_A = 128
GROUP_A = HEADS_A // KV_HEADS_A

HEADS_B = 8
Q_LORA = 384
KV_LORA = 256
NOPE_B = 128
ROPE_B = 64
V_B = 128
QK_B = NOPE_B + ROPE_B

LANES = 128
HALF = LANES // 2
QK_B_PAD = 2 * LANES
META_PAD = LANES
VMEM_LIMIT_BYTES = 56 * 1024 * 1024

C_QA = 0
C_KA = C_QA + HEADS_A * HEAD_DIM_A
C_VA = C_KA + KV_HEADS_A * HEAD_DIM_A
C_CQ = C_VA + KV_HEADS_A * HEAD_DIM_A
C_CKV = C_CQ + Q_LORA
C_KR = C_CKV + KV_LORA
C_QKV_END = C_KR + LANES
N_GATES = 2 * D_MODEL

NEG_BIG = -1e30
LOG2_E = 1.4426950408889634
MAX_UNSHIFTED_LOGIT = 60.0
UNROLLED_QUERY_ROWS = 8192


def _rotary_lanes(rot_dim):
    q = rot_dim // 4
    first = np.concatenate([np.arange(q), 2 * q + np.arange(q)])
    pad = np.full(HALF - 2 * q, -1)
    return np.concatenate([first, pad, first + q, pad])


LANES_ROPE_A = _rotary_lanes(HEAD_DIM_A)
LANES_ROPE_B = _rotary_lanes(ROPE_B)


def _rms_scale(x, dim):
    return lax.rsqrt(jnp.sum(x * x, axis=-1, keepdims=True) * (1.0 / dim) + EPS)


def _rope(y, tab_ref, rs):
    return y * tab_ref[0, rs, :] + pltpu.roll(y, HALF, 1) * tab_ref[1, rs, :]


def _normed_input(x):
    return x * _rms_scale(x, D_MODEL)


def _qkv_kernel(x_ref, gain_ref, w_in_ref, w_uq_ref, w_ukv_ref, qna_ref, kna_ref, cqn_ref, ckvn_ref,
                qnb_ref, knb_ref, rope_a_ref, rope_b_ref,
                qa_ref, ka_ref, va_ref, qb_ref, kb_ref, vb_ref, *, n_sub):
    tms = x_ref.shape[1] // n_sub
    gq = qna_ref[...] * (HEAD_DIM_A ** -0.5 * LOG2_E)
    gqb = qnb_ref[...] * (QK_B ** -0.5 * LOG2_E)
    gk_nope = knb_ref[:, :LANES]
    gk_rope = knb_ref[:, LANES:]

    for t in range(n_sub):
        rs = pl.ds(t * tms, tms)
        u = (_normed_input(x_ref[0, rs, :]) * gain_ref[...]).astype(BF16)

        def proj(c0, c1):
            return jnp.dot(u, w_in_ref[:, c0:c1], preferred_element_type=F32)

        zq = proj(C_QA, C_KA)
        for h in range(HEADS_A):
            xh = zq[:, h * LANES:(h + 1) * LANES]
            y = xh * _rms_scale(xh, HEAD_DIM_A) * gq
            qa_ref[0, h, rs, :] = _rope(y, rope_a_ref, rs).astype(BF16)
        zk = proj(C_KA, C_VA)
        for h in range(KV_HEADS_A):
            xh = zk[:, h * LANES:(h + 1) * LANES]
            y = xh * _rms_scale(xh, HEAD_DIM_A) * kna_ref[...]
            ka_ref[0, h, rs, :] = _rope(y, rope_a_ref, rs).astype(BF16)
        zv = proj(C_VA, C_CQ)
        for h in range(KV_HEADS_A):
            va_ref[0, h, rs, :] = zv[:, h * LANES:(h + 1) * LANES].astype(BF16)


        cq = proj(C_CQ, C_CKV)
        eps_q = EPS * (jnp.sum(cq * cq, axis=-1, keepdims=True) * (1.0 / Q_LORA) + EPS)
        zqb = jnp.dot((cq * cqn_ref[...]).astype(BF16), w_uq_ref[...], preferred_element_type=F32)
        for h in range(HEADS_B):
            xh = zqb[:, h * QK_B_PAD:(h + 1) * QK_B_PAD]
            y = xh * lax.rsqrt(jnp.sum(xh * xh, axis=-1, keepdims=True) * (1.0 / QK_B) + eps_q) * gqb
            qb_ref[0, h, rs, :LANES] = y[:, :LANES].astype(BF16)
            qb_ref[0, h, rs, LANES:] = _rope(y[:, LANES:], rope_b_ref, rs).astype(BF16)

        ckv = proj(C_CKV, C_KR)
        r_kv = _rms_scale(ckv, KV_LORA)
        zkv = jnp.dot((ckv * ckvn_ref[...]).astype(BF16), w_ukv_ref[...], preferred_element_type=F32)
        kr = proj(C_KR, C_QKV_END)
        kr_ss = jnp.sum(kr * kr, axis=-1, keepdims=True)
        kr_rot = _rope(kr * gk_rope, rope_b_ref, rs)
        for h in range(HEADS_B):
            kn = zkv[:, h * 2 * LANES:h * 2 * LANES + LANES]
            ss = jnp.sum(kn * kn, axis=-1, keepdims=True) * (r_kv * r_kv) + kr_ss
            r = lax.rsqrt(ss * (1.0 / QK_B) + EPS)
            kb_ref[0, h, rs, :LANES] = (kn * (r * r_kv) * gk_nope).astype(BF16)
            kb_ref[0, h, rs, LANES:] = (kr_rot * r).astype(BF16)
            vb_ref[0, h, rs, :] = (zkv[:, h * 2 * LANES + LANES:(h + 1) * 2 * LANES] * r_kv).astype(BF16)


def _const_spec(shape):
    return pl.BlockSpec(shape, lambda *_: (0,) * len(shape))


def _project_qkv(x, tm, n_sub, gain, w_qkv, w_uq, w_ukv, qna, kna, cqn, ckvn, qnb, knb, rope_a, rope_b):
    b, n, _ = x.shape

    def head_out(heads, width):
        return (jax.ShapeDtypeStruct((b, heads, n, width), BF16),
                pl.BlockSpec((1, heads, tm, width), lambda bi, i: (bi, 0, i, 0)))

    outs = [head_out(HEADS_A, HEAD_DIM_A), head_out(KV_HEADS_A, HEAD_DIM_A), head_out(KV_HEADS_A, HEAD_DIM_A),
            head_out(HEADS_B, QK_B_PAD), head_out(HEADS_B, QK_B_PAD), head_out(HEADS_B, V_B)]
    consts = (gain, w_qkv, w_uq, w_ukv, qna, kna, cqn, ckvn, qnb, knb)
    rope_spec = pl.BlockSpec((2, tm, LANES), lambda bi, i: (0, i, 0))
    return pl.pallas_call(
        functools.partial(_qkv_kernel, n_sub=n_sub),
        grid=(b, n // tm),
        in_specs=[pl.BlockSpec((1, tm, D_MODEL), lambda bi, i: (bi, i, 0))]
        + [_const_spec(c.shape) for c in consts] + [rope_spec, rope_spec],
        out_specs=[o[1] for o in outs],
        out_shape=[o[0] for o in outs],
        compiler_params=pltpu.CompilerParams(
            dimension_semantics=("arbitrary", "arbitrary"), vmem_limit_bytes=VMEM_LIMIT_BYTES),
        name="qkv_proj",
    )(x, *consts, rope_a, rope_b)


def _gates_kernel(x_ref, gain_ref, w_ref, o_ref):
    u = (_normed_input(x_ref[0]) * gain_ref[...]).astype(BF16)
    o_ref[0] = jnp.dot(u, w_ref[...], preferred_element_type=F32)


def _project_gates(x, tm, gain, w_gates):
    b, n, _ = x.shape
    return pl.pallas_call(
        _gates_kernel,
        grid=(b, n // tm),
        in_specs=[pl.BlockSpec((1, tm, D_MODEL), lambda bi, i: (bi, i, 0)),
                  _const_spec(gain.shape), _const_spec(w_gates.shape)],
        out_specs=pl.BlockSpec((1, tm, N_GATES), lambda bi, i: (bi, i, 0)),
        out_shape=jax.ShapeDtypeStruct((b, n, N_GATES), F32),
        compiler_params=pltpu.CompilerParams(
            dimension_semantics=("arbitrary", "arbitrary"), vmem_limit_bytes=VMEM_LIMIT_BYTES),
        name="gate_proj",
    )(x, gain, w_gates)


def _lane_block_sum(p):
    out = p[:, :LANES]
    for c in range(1, p.shape[1] // LANES):
        out = out + p[:, c * LANES:(c + 1) * LANES]
    return out


def _attn_kernel(safe_ref, q_ref, k_ref, v_ref, km_ref, vm_ref, g_ref, o_ref, m_sc, l_sc, acc_sc,
                 *, group, tq, n_sub, tk, tk_online):
    tqs = tq // n_sub
    ms = group * tqs
    n = k_ref.shape[2]
    nt = (((1,), (1,)), ((), ()))
    subs = range(n_sub)
    rows = [pl.ds(s * ms, ms) for s in subs]
    is_meta = lax.broadcasted_iota(jnp.int32, (ms, META_PAD), 1) < N_META

    def q_sub(s):
        return q_ref[0, :, s * tqs:(s + 1) * tqs, :].reshape(ms, q_ref.shape[-1])

    def meta_logits(s):
        return lax.dot_general(q_sub(s), km_ref[0], nt, preferred_element_type=F32)

    def kv_tile(j, width):
        off = pl.multiple_of(j * width, width)
        return k_ref[0, 0, pl.ds(off, width), :], v_ref[0, 0, pl.ds(off, width), :]

    def finish(s, l):
        o = acc_sc[rows[s], :] / l
        for h in range(group):
            g = g_ref[0, s * tqs:(s + 1) * tqs, h * LANES:(h + 1) * LANES]
            o_ref[0, s * tqs:(s + 1) * tqs, h * LANES:(h + 1) * LANES] = (
                o[h * tqs:(h + 1) * tqs] * (g * jax.nn.sigmoid(g))).astype(BF16)

    @pl.when(safe_ref[0] != 0)
    def _():
        for s in subs:
            p = jnp.where(is_meta, jnp.exp2(meta_logits(s)), 0.0)
            l_sc[rows[s], :] = p
            acc_sc[rows[s], :] = jnp.dot(p.astype(BF16), vm_ref[0], preferred_element_type=F32)

        def body(j, carry):
            k, v = kv_tile(j, tk)
            for s in subs:
                p = jnp.exp2(lax.dot_general(q_sub(s), k, nt, preferred_element_type=F32))
                l_sc[rows[s], :] += _lane_block_sum(p)
                acc_sc[rows[s], :] += jnp.dot(p.astype(BF16), v, preferred_element_type=F32)
            return carry

        lax.fori_loop(0, n // tk, body, 0, unroll=min(n // tk, max(1, UNROLLED_QUERY_ROWS // (n_sub * ms))))
        for s in subs:
            finish(s, jnp.sum(l_sc[rows[s], :], axis=1, keepdims=True))

    @pl.when(safe_ref[0] == 0)
    def _():
        for s in subs:
            sm = jnp.where(is_meta, meta_logits(s), NEG_BIG)
            m0 = jnp.max(sm, axis=1, keepdims=True)
            p = jnp.exp2(sm - m0)
            m_sc[rows[s], :] = jnp.broadcast_to(m0, (ms, LANES))
            l_sc[rows[s], :] = jnp.broadcast_to(jnp.sum(p, axis=1, keepdims=True), (ms, LANES))
            acc_sc[rows[s], :] = jnp.dot(p.astype(BF16), vm_ref[0], preferred_element_type=F32)

        def body(j, carry):
            k, v = kv_tile(j, tk_online)
            for s in subs:
                sc = lax.dot_general(q_sub(s), k, nt, preferred_element_type=F32)
                m_prev = m_sc[rows[s], :]
                m_new = jnp.maximum(m_prev, jnp.max(sc, axis=1, keepdims=True))
                alpha = jnp.exp2(m_prev - m_new)
                p = jnp.exp2(sc - jnp.concatenate([m_new] * (tk_online // LANES), axis=1))
                l_sc[rows[s], :] = alpha * l_sc[rows[s], :] + jnp.sum(p, axis=1, keepdims=True)
                acc_sc[rows[s], :] = alpha * acc_sc[rows[s], :] + jnp.dot(
                    p.astype(BF16), v, preferred_element_type=F32)
                m_sc[rows[s], :] = m_new
            return carry

        lax.fori_loop(0, n // tk_online, body, 0)
        for s in subs:
            finish(s, l_sc[rows[s], :])


def _attention(safe, q, k, v, k_meta, v_meta, gates, *, group, gate_col, tq, n_sub, tk, tk_online):
    b, hq, n, dk = q.shape
    hkv = k.shape[1]
    dv = v.shape[-1]
    width = group * dv
    assert n % tq == 0 and tq % n_sub == 0 and n % tk == 0 and n % tk_online == 0, (n, tq, n_sub, tk, tk_online)
    gate_blk = gate_col // width
    kernel = functools.partial(_attn_kernel, group=group, tq=tq, n_sub=n_sub, tk=tk, tk_online=tk_online)
    grid_spec = pltpu.PrefetchScalarGridSpec(
        num_scalar_prefetch=1,
        grid=(b, hkv, n // tq),
        in_specs=[
            pl.BlockSpec((1, group, tq, dk), lambda bi, g, i, s: (bi, g, i, 0)),
            pl.BlockSpec((1, 1, n, dk), lambda bi, g, i, s: (bi, g, 0, 0)),
            pl.BlockSpec((1, 1, n, dv), lambda bi, g, i, s: (bi, g, 0, 0)),
            pl.BlockSpec((1, META_PAD, dk), lambda bi, g, i, s: (g, 0, 0)),
            pl.BlockSpec((1, META_PAD, dv), lambda bi, g, i, s: (g, 0, 0)),
            pl.BlockSpec((1, tq, width), lambda bi, g, i, s: (bi, i, gate_blk + g)),
        ],
        out_specs=pl.BlockSpec((1, tq, width), lambda bi, g, i, s: (bi, i, g)),
        scratch_shapes=[pltpu.VMEM((group * tq, LANES), F32)] * 2 + [pltpu.VMEM((group * tq, dv), F32)],
    )
    return pl.pallas_call(
        kernel,
        grid_spec=grid_spec,
        out_shape=jax.ShapeDtypeStruct((b, n, hq * dv), BF16),
        compiler_params=pltpu.CompilerParams(
            dimension_semantics=("arbitrary", "arbitrary", "arbitrary"), vmem_limit_bytes=VMEM_LIMIT_BYTES),
        name="attn_g%d" % group,
    )(safe, q, k, v, k_meta, v_meta, gates)


def _logits_bounded(g_q, g_k, head_dim):
    bound = jnp.max(jnp.abs(g_q)) * jnp.max(jnp.abs(g_k)) * (head_dim ** 0.5)
    return (bound <= MAX_UNSHIFTED_LOGIT).astype(jnp.int32).reshape(1)


def _out_kernel(x_ref, oa_ref, ob_ref, gain_ref, wm_ref, bm_ref, wpa_ref, wpb_ref, wo_ref, y_ref):
    x = x_ref[0]
    u = (_normed_input(x) * gain_ref[...]).astype(BF16)
    m = jnp.dot(u, wm_ref[...], preferred_element_type=F32)
    ya = jnp.dot(oa_ref[0], wpa_ref[...], preferred_element_type=F32)
    yb = jnp.dot(ob_ref[0], wpb_ref[...], preferred_element_type=F32)
    mix = (jax.nn.sigmoid(m[:, :D_MODEL] + bm_ref[0:1]) * ya
           + jax.nn.sigmoid(m[:, D_MODEL:] + bm_ref[1:2]) * yb)
    y_ref[0] = x + jnp.dot(mix.astype(BF16), wo_ref[...], preferred_element_type=F32)


def _output(x, oa, ob, gain, w_merge, b_merge, w_proj_a, w_proj_b, w_out, tm):
    b, n, _ = x.shape
    tok = pl.BlockSpec((1, tm, D_MODEL), lambda bi, i: (bi, i, 0))
    consts = (gain, w_merge, b_merge, w_proj_a, w_proj_b, w_out)
    return pl.pallas_call(
        _out_kernel,
        grid=(b, n // tm),
        in_specs=[tok, tok, tok] + [_const_spec(c.shape) for c in consts],
        out_specs=tok,
        out_shape=jax.ShapeDtypeStruct(x.shape, F32),
        compiler_params=pltpu.CompilerParams(
            dimension_semantics=("arbitrary", "arbitrary"), vmem_limit_bytes=VMEM_LIMIT_BYTES),
        name="out_proj",
    )(x, oa, ob, *consts)


def _rope_tables(n, rot_dim):
    rows = n // GRID_W
    axis_dim = rot_dim // 2
    inv_freq = ROPE_THETA ** (-jnp.arange(0, axis_dim, 2, dtype=F32) / axis_dim)
    ang_r = jnp.arange(rows, dtype=F32)[:, None] * inv_freq[None, :]
    ang_c = jnp.arange(GRID_W, dtype=F32)[:, None] * inv_freq[None, :]
    cr, sr, cc, sc = jnp.cos(ang_r), jnp.sin(ang_r), jnp.cos(ang_c), jnp.sin(ang_c)
    zr, zc = jnp.zeros_like(cr), jnp.zeros_like(cc)
    pad_r, pad_c = (jnp.zeros((m, HALF - rot_dim // 2), F32) for m in (rows, GRID_W))
    by_row = jnp.stack([jnp.concatenate([cr, zr, pad_r, cr, zr, pad_r], axis=1),
                        jnp.concatenate([-sr, zr, pad_r, sr, zr, pad_r], axis=1)])
    by_col = jnp.stack([jnp.concatenate([zc, cc, pad_c, zc, cc, pad_c], axis=1),
                        jnp.concatenate([zc, -sc, pad_c, zc, sc, pad_c], axis=1)])
    return (by_row[:, :, None, :] + by_col[:, None, :, :]).reshape(2, n, LANES)


def _identity_rope(n):
    return jnp.stack([jnp.ones((n, LANES), F32), jnp.zeros((n, LANES), F32)])


def _gather_cols(w, idx):
    pieces, i = [], 0
    while i < len(idx):
        j = i + 1
        if idx[i] < 0:
            while j < len(idx) and idx[j] < 0:
                j += 1
            pieces.append(jnp.zeros((w.shape[0], j - i), w.dtype))
        else:
            while j < len(idx) and idx[j] == idx[j - 1] + 1:
                j += 1
            pieces.append(w[:, int(idx[i]):int(idx[j - 1]) + 1])
        i = j
    return jnp.concatenate(pieces, axis=1)


def _pad_meta(a):
    return jnp.pad(a[0], ((0, 0), (0, META_PAD - N_META), (0, 0)))


def _tile(n, pref):
    return pref if n % pref == 0 else n


def kernel(x_prompt, x_sample, meta_tokens, norm_gain, w_in, b_merge, q_norm_a, k_norm_a, cq_norm, ckv_norm,
           w_uq, w_ukv, q_norm_b, k_norm_b, w_proj_a, w_proj_b, w_out):
    assert norm_gain.shape[0] == 1, "meta-token queries are only skipped for a single layer"
    wi = w_in[0].astype(BF16)
    s0 = HEADS_A * HEAD_DIM_A
    s1 = s0 + 2 * KV_HEADS_A * HEAD_DIM_A
    s2 = s1 + HEADS_A * HEAD_DIM_A
    s3 = s2 + Q_LORA + KV_LORA
    s4 = s3 + ROPE_B
    rope_b_cols = np.where(LANES_ROPE_B < 0, -1, NOPE_B + LANES_ROPE_B)
    qk_a_cols = np.concatenate([h * HEAD_DIM_A + LANES_ROPE_A for h in range(HEADS_A + KV_HEADS_A)])
    qkv_cols = np.concatenate([qk_a_cols, np.arange(s0 + KV_HEADS_A * HEAD_DIM_A, s1), np.arange(s2, s3),
                               np.where(LANES_ROPE_B < 0, -1, s3 + LANES_ROPE_B)])
    w_qkv = _gather_cols(wi, qkv_cols)
    w_gates = jnp.concatenate([wi[:, s1:s2], wi[:, s4:s4 + D_MODEL]], axis=1)
    w_merge = wi[:, s4 + D_MODEL:]
    head_b_cols = np.concatenate([np.arange(NOPE_B), rope_b_cols])
    uq_cols = np.concatenate([np.where(head_b_cols < 0, -1, h * QK_B + head_b_cols) for h in range(HEADS_B)])
    w_uq_p = _gather_cols(w_uq[0].astype(BF16), uq_cols)
    w_ukv_b = w_ukv[0].astype(BF16)
    row = lambda a: a.reshape(1, -1).astype(F32)
    norms = (_gather_cols(row(q_norm_a[0]), LANES_ROPE_A), _gather_cols(row(k_norm_a[0]), LANES_ROPE_A),
             row(cq_norm[0]), row(ckv_norm[0]),
             _gather_cols(row(q_norm_b[0]), head_b_cols), _gather_cols(row(k_norm_b[0]), head_b_cols))
    gain = row(norm_gain[0])
    proj_w = (gain, w_qkv, w_uq_p, w_ukv_b) + norms
    wpa, wpb, wo = w_proj_a[0].astype(BF16), w_proj_b[0].astype(BF16), w_out[0].astype(BF16)
    bm = b_merge[0].astype(F32)

    meta = _project_qkv(meta_tokens[None].astype(F32), N_META, 1, *proj_w,
                        _identity_rope(N_META), _identity_rope(N_META))
    ka_m, va_m, kb_m, vb_m = (_pad_meta(meta[i]) for i in (1, 2, 4, 5))
    safe_a = _logits_bounded(q_norm_a[0], k_norm_a[0], HEAD_DIM_A)
    safe_b = _logits_bounded(q_norm_b[0], k_norm_b[0], QK_B)

    n_max = max(x_prompt.shape[1], x_sample.shape[1])
    rope_a, rope_b = _rope_tables(n_max, HEAD_DIM_A), _rope_tables(n_max, ROPE_B)

    def encode(x):
        n = x.shape[1]
        qa, ka, va, qb, kb, vb = _project_qkv(x, _tile(n, 512), 2, *proj_w, rope_a, rope_b)
        gates = _project_gates(x, _tile(n, 1024), gain, w_gates)
        oa = _attention(safe_a, qa, ka, va, ka_m, va_m, gates, group=GROUP_A, gate_col=0,
                        tq=_tile(n, 1024), n_sub=8, tk=_tile(n, 2048), tk_online=_tile(n, 512))
        tq_b = min(n, 4096)
        ob = _attention(safe_b, qb, kb, vb, kb_m, vb_m, gates, group=1, gate_col=D_MODEL,
                        tq=tq_b, n_sub=tq_b // 1024, tk=_tile(n, 2048), tk_online=_tile(n, 512))
        return _output(x, oa, ob, gain, w_merge, bm, wpa, wpb, wo, _tile(n, 512))

    return (encode(x_prompt), encode(x_sample))
```

```python
import functools

import numpy as np
import jax
import jax.numpy as jnp
from jax import lax
from jax.experimental import pallas as pl
from jax.experimental.pallas import tpu as pltpu

F32 = jnp.float32
BF16 = jnp.bfloat16

D_MODEL = 1024
N_META = 16
GRID_W = 64
ROPE_THETA = 10000.0
EPS = 1e-6

HEADS_A = 8
KV_HEADS_A = 2
HEAD_DIM_A = 128
GROUP_A = HEADS_A // KV_HEADS_A

HEADS_B = 8
Q_LORA = 384
KV_LORA = 256
NOPE_B = 128
ROPE_B = 64
V_B = 128
QK_B = NOPE_B + ROPE_B

LANES = 128
HALF = LANES // 2
QK_B_PAD = 2 * LANES
META_PAD = LANES
VMEM_LIMIT_BYTES = 56 * 1024 * 1024

C_QA = 0
C_KA = C_QA + HEADS_A * HEAD_DIM_A
C_VA = C_KA + KV_HEADS_A * HEAD_DIM_A
C_CQ = C_VA + KV_HEADS_A * HEAD_DIM_A
C_CKV = C_CQ + Q_LORA
C_KR = C_CKV + KV_LORA
C_QKV_END = C_KR + LANES
N_GATES = 2 * D_MODEL

NEG_BIG = -1e30
LOG2_E = 1.4426950408889634
MAX_UNSHIFTED_LOGIT = 60.0
UNROLLED_QUERY_ROWS = 8192


def _rotary_lanes(rot_dim):
    q = rot_dim // 4
    first = np.concatenate([np.arange(q), 2 * q + np.arange(q)])
    pad = np.full(HALF - 2 * q, -1)
    return np.concatenate([first, pad, first + q, pad])


LANES_ROPE_A = _rotary_lanes(HEAD_DIM_A)
LANES_ROPE_B = _rotary_lanes(ROPE_B)


def _rms_scale(x, dim):
    return lax.rsqrt(jnp.sum(x * x, axis=-1, keepdims=True) * (1.0 / dim) + EPS)


def _rope(y, tab_ref, rs):
    return y * tab_ref[0, rs, :] + pltpu.roll(y, HALF, 1) * tab_ref[1, rs, :]


def _normed_input(x):
    return x * _rms_scale(x, D_MODEL)


def _qkv_kernel(x_ref, gain_ref, w_in_ref, w_uq_ref, w_ukv_ref, qna_ref, kna_ref, cqn_ref, ckvn_ref,
                qnb_ref, knb_ref, wg_ref, rope_a_ref, rope_b_ref,
                qa_ref, ka_ref, va_ref, qb_ref, kb_ref, vb_ref, gates_ref, *, n_sub):
    tms = x_ref.shape[1] // n_sub
    gq = qna_ref[...] * (HEAD_DIM_A ** -0.5 * LOG2_E)
    gqb = qnb_ref[...] * (QK_B ** -0.5 * LOG2_E)
    gk_nope = knb_ref[:, :LANES]
    gk_rope = knb_ref[:, LANES:]

    for t in range(n_sub):
        rs = pl.ds(t * tms, tms)
        u = (_normed_input(x_ref[0, rs, :]) * gain_ref[...]).astype(BF16)

        def proj(c0, c1):
            return jnp.dot(u, w_in_ref[:, c0:c1], preferred_element_type=F32)

        zq = proj(C_QA, C_KA)
        for h in range(HEADS_A):
            xh = zq[:, h * LANES:(h + 1) * LANES]
            y = xh * _rms_scale(xh, HEAD_DIM_A) * gq
            qa_ref[0, h, rs, :] = _rope(y, rope_a_ref, rs).astype(BF16)
        zk = proj(C_KA, C_VA)
        for h in range(KV_HEADS_A):
            xh = zk[:, h * LANES:(h + 1) * LANES]
            y = xh * _rms_scale(xh, HEAD_DIM_A) * kna_ref[...]
            ka_ref[0, h, rs, :] = _rope(y, rope_a_ref, rs).astype(BF16)
        zv = proj(C_VA, C_CQ)
        for h in range(KV_HEADS_A):
            va_ref[0, h, rs, :] = zv[:, h * LANES:(h + 1) * LANES].astype(BF16)


        cq = proj(C_CQ, C_CKV)
        eps_q = EPS * (jnp.sum(cq * cq, axis=-1, keepdims=True) * (1.0 / Q_LORA) + EPS)
        zqb = jnp.dot((cq * cqn_ref[...]).astype(BF16), w_uq_ref[...], preferred_element_type=F32)
        for h in range(HEADS_B):
            xh = zqb[:, h * QK_B_PAD:(h + 1) * QK_B_PAD]
            y = xh * lax.rsqrt(jnp.sum(xh * xh, axis=-1, keepdims=True) * (1.0 / QK_B) + eps_q) * gqb
            qb_ref[0, h, rs, :LANES] = y[:, :LANES].astype(BF16)
            qb_ref[0, h, rs, LANES:] = _rope(y[:, LANES:], rope_b_ref, rs).astype(BF16)

        ckv = proj(C_CKV, C_KR)
        r_kv = _rms_scale(ckv, KV_LORA)
        zkv = jnp.dot((ckv * ckvn_ref[...]).astype(BF16), w_ukv_ref[...], preferred_element_type=F32)
        kr = proj(C_KR, C_QKV_END)
        kr_ss = jnp.sum(kr * kr, axis=-1, keepdims=True)
        kr_rot = _rope(kr * gk_rope, rope_b_ref, rs)
        for h in range(HEADS_B):
            kn = zkv[:, h * 2 * LANES:h * 2 * LANES + LANES]
            ss = jnp.sum(kn * kn, axis=-1, keepdims=True) * (r_kv * r_kv) + kr_ss
            r = lax.rsqrt(ss * (1.0 / QK_B) + EPS)
            kb_ref[0, h, rs, :LANES] = (kn * (r * r_kv) * gk_nope).astype(BF16)
            kb_ref[0, h, rs, LANES:] = (kr_rot * r).astype(BF16)
            vb_ref[0, h, rs, :] = (zkv[:, h * 2 * LANES + LANES:(h + 1) * 2 * LANES] * r_kv).astype(BF16)

        gates_ref[0, rs, :] = jnp.dot(u, wg_ref[...], preferred_element_type=F32)


def _const_spec(shape):
    return pl.BlockSpec(shape, lambda *_: (0,) * len(shape))


def _project_qkv(x, tm, n_sub, gain, w_qkv, w_uq, w_ukv, qna, kna, cqn, ckvn, qnb, knb, wg, rope_a, rope_b):
    b, n, _ = x.shape

    def head_out(heads, width):
        return (jax.ShapeDtypeStruct((b, heads, n, width), BF16),
                pl.BlockSpec((1, heads, tm, width), lambda bi, i: (bi, 0, i, 0)))

    outs = [head_out(HEADS_A, HEAD_DIM_A), head_out(KV_HEADS_A, HEAD_DIM_A), head_out(KV_HEADS_A, HEAD_DIM_A),
            head_out(HEADS_B, QK_B_PAD), head_out(HEADS_B, QK_B_PAD), head_out(HEADS_B, V_B),
            (jax.ShapeDtypeStruct((b, n, N_GATES), F32), pl.BlockSpec((1, tm, N_GATES), lambda bi, i: (bi, i, 0)))]
    consts = (gain, w_qkv, w_uq, w_ukv, qna, kna, cqn, ckvn, qnb, knb, wg)
    rope_spec = pl.BlockSpec((2, tm, LANES), lambda bi, i: (0, i, 0))
    return pl.pallas_call(
        functools.partial(_qkv_kernel, n_sub=n_sub),
        grid=(b, n // tm),
        in_specs=[pl.BlockSpec((1, tm, D_MODEL), lambda bi, i: (bi, i, 0))]
        + [_const_spec(c.shape) for c in consts] + [rope_spec, rope_spec],
        out_specs=[o[1] for o in outs],
        out_shape=[o[0] for o in outs],
        compiler_params=pltpu.CompilerParams(
            dimension_semantics=("arbitrary", "arbitrary"), vmem_limit_bytes=VMEM_LIMIT_BYTES),
        name="qkv_proj",
    )(x, *consts, rope_a, rope_b)


def _gates_kernel(x_ref, gain_ref, w_ref, o_ref):
    u = (_normed_input(x_ref[0]) * gain_ref[...]).astype(BF16)
    o_ref[0] = jnp.dot(u, w_ref[...], preferred_element_type=F32)


def _project_gates(x, tm, gain, w_gates):
    b, n, _ = x.shape
    return pl.pallas_call(
        _gates_kernel,
        grid=(b, n // tm),
        in_specs=[pl.BlockSpec((1, tm, D_MODEL), lambda bi, i: (bi, i, 0)),
                  _const_spec(gain.shape), _const_spec(w_gates.shape)],
        out_specs=pl.BlockSpec((1, tm, N_GATES), lambda bi, i: (bi, i, 0)),
        out_shape=jax.ShapeDtypeStruct((b, n, N_GATES), F32),
        compiler_params=pltpu.CompilerParams(
            dimension_semantics=("arbitrary", "arbitrary"), vmem_limit_bytes=VMEM_LIMIT_BYTES),
        name="gate_proj",
    )(x, gain, w_gates)


def _lane_block_sum(p):
    out = p[:, :LANES]
    for c in range(1, p.shape[1] // LANES):
        out = out + p[:, c * LANES:(c + 1) * LANES]
    return out


def _attn_kernel(safe_ref, q_ref, k_ref, v_ref, km_ref, vm_ref, g_ref, o_ref, m_sc, l_sc, acc_sc,
                 *, group, tq, n_sub, tk, tk_online):
    tqs = tq // n_sub
    ms = group * tqs
    n = k_ref.shape[2]
    nt = (((1,), (1,)), ((), ()))
    subs = range(n_sub)
    rows = [pl.ds(s * ms, ms) for s in subs]
    is_meta = lax.broadcasted_iota(jnp.int32, (ms, META_PAD), 1) < N_META

    def q_sub(s):
        return q_ref[0, :, s * tqs:(s + 1) * tqs, :].reshape(ms, q_ref.shape[-1])

    def meta_logits(s):
        return lax.dot_general(q_sub(s), km_ref[0], nt, preferred_element_type=F32)

    def kv_tile(j, width):
        off = pl.multiple_of(j * width, width)
        return k_ref[0, 0, pl.ds(off, width), :], v_ref[0, 0, pl.ds(off, width), :]

    def finish(s, l):
        o = acc_sc[rows[s], :] / l
        for h in range(group):
            g = g_ref[0, s * tqs:(s + 1) * tqs, h * LANES:(h + 1) * LANES]
            o_ref[0, s * tqs:(s + 1) * tqs, h * LANES:(h + 1) * LANES] = (
                o[h * tqs:(h + 1) * tqs] * (g * jax.nn.sigmoid(g))).astype(BF16)

    @pl.when(safe_ref[0] != 0)
    def _():
        for s in subs:
            p = jnp.where(is_meta, jnp.exp2(meta_logits(s)), 0.0)
            l_sc[rows[s], :] = p
            acc_sc[rows[s], :] = jnp.dot(p.astype(BF16), vm_ref[0], preferred_element_type=F32)

        def body(j, carry):
            k, v = kv_tile(j, tk)
            for s in subs:
                p = jnp.exp2(lax.dot_general(q_sub(s), k, nt, preferred_element_type=F32))
                l_sc[rows[s], :] += _lane_block_sum(p)
                acc_sc[rows[s], :] += jnp.dot(p.astype(BF16), v, preferred_element_type=F32)
            return carry

        lax.fori_loop(0, n // tk, body, 0, unroll=min(n // tk, max(1, UNROLLED_QUERY_ROWS // (n_sub * ms))))
        for s in subs:
            finish(s, jnp.sum(l_sc[rows[s], :], axis=1, keepdims=True))

    @pl.when(safe_ref[0] == 0)
    def _():
        for s in subs:
            sm = jnp.where(is_meta, meta_logits(s), NEG_BIG)
            m0 = jnp.max(sm, axis=1, keepdims=True)
            p = jnp.exp2(sm - m0)
            m_sc[rows[s], :] = jnp.broadcast_to(m0, (ms, LANES))
            l_sc[rows[s], :] = jnp.broadcast_to(jnp.sum(p, axis=1, keepdims=True), (ms, LANES))
            acc_sc[rows[s], :] = jnp.dot(p.astype(BF16), vm_ref[0], preferred_element_type=F32)

        def body(j, carry):
            k, v = kv_tile(j, tk_online)
            for s in subs:
                sc = lax.dot_general(q_sub(s), k, nt, preferred_element_type=F32)
                m_prev = m_sc[rows[s], :]
                m_new = jnp.maximum(m_prev, jnp.max(sc, axis=1, keepdims=True))
                alpha = jnp.exp2(m_prev - m_new)
                p = jnp.exp2(sc - jnp.concatenate([m_new] * (tk_online // LANES), axis=1))
                l_sc[rows[s], :] = alpha * l_sc[rows[s], :] + jnp.sum(p, axis=1, keepdims=True)
                acc_sc[rows[s], :] = alpha * acc_sc[rows[s], :] + jnp.dot(
                    p.astype(BF16), v, preferred_element_type=F32)
                m_sc[rows[s], :] = m_new
            return carry

        lax.fori_loop(0, n // tk_online, body, 0)
        for s in subs:
            finish(s, l_sc[rows[s], :])


def _attention(safe, q, k, v, k_meta, v_meta, gates, *, group, gate_col, tq, n_sub, tk, tk_online):
    b, hq, n, dk = q.shape
    hkv = k.shape[1]
    dv = v.shape[-1]
    width = group * dv
    assert n % tq == 0 and tq % n_sub == 0 and n % tk == 0 and n % tk_online == 0, (n, tq, n_sub, tk, tk_online)
    gate_blk = gate_col // width
    kernel = functools.partial(_attn_kernel, group=group, tq=tq, n_sub=n_sub, tk=tk, tk_online=tk_online)
    grid_spec = pltpu.PrefetchScalarGridSpec(
        num_scalar_prefetch=1,
        grid=(b, hkv, n // tq),
        in_specs=[
            pl.BlockSpec((1, group, tq, dk), lambda bi, g, i, s: (bi, g, i, 0)),
            pl.BlockSpec((1, 1, n, dk), lambda bi, g, i, s: (bi, g, 0, 0)),
            pl.BlockSpec((1, 1, n, dv), lambda bi, g, i, s: (bi, g, 0, 0)),
            pl.BlockSpec((1, META_PAD, dk), lambda bi, g, i, s: (g, 0, 0)),
            pl.BlockSpec((1, META_PAD, dv), lambda bi, g, i, s: (g, 0, 0)),
            pl.BlockSpec((1, tq, width), lambda bi, g, i, s: (bi, i, gate_blk + g)),
        ],
        out_specs=pl.BlockSpec((1, tq, width), lambda bi, g, i, s: (bi, i, g)),
        scratch_shapes=[pltpu.VMEM((group * tq, LANES), F32)] * 2 + [pltpu.VMEM((group * tq, dv), F32)],
    )
    return pl.pallas_call(
        kernel,
        grid_spec=grid_spec,
        out_shape=jax.ShapeDtypeStruct((b, n, hq * dv), BF16),
        compiler_params=pltpu.CompilerParams(
            dimension_semantics=("arbitrary", "arbitrary", "arbitrary"), vmem_limit_bytes=VMEM_LIMIT_BYTES),
        name="attn_g%d" % group,
    )(safe, q, k, v, k_meta, v_meta, gates)


def _logits_bounded(g_q, g_k, head_dim):
    bound = jnp.max(jnp.abs(g_q)) * jnp.max(jnp.abs(g_k)) * (head_dim ** 0.5)
    return (bound <= MAX_UNSHIFTED_LOGIT).astype(jnp.int32).reshape(1)


def _out_kernel(x_ref, oa_ref, ob_ref, gain_ref, wm_ref, bm_ref, wpa_ref, wpb_ref, wo_ref, y_ref):
    x = x_ref[0]
    u = (_normed_input(x) * gain_ref[...]).astype(BF16)
    m = jnp.dot(u, wm_ref[...], preferred_element_type=F32)
    ya = jnp.dot(oa_ref[0], wpa_ref[...], preferred_element_type=F32)
    yb = jnp.dot(ob_ref[0], wpb_ref[...], preferred_element_type=F32)
    mix = (jax.nn.sigmoid(m[:, :D_MODEL] + bm_ref[0:1]) * ya
           + jax.nn.sigmoid(m[:, D_MODEL:] + bm_ref[1:2]) * yb)
    y_ref[0] = x + jnp.dot(mix.astype(BF16), wo_ref[...], preferred_element_type=F32)


def _output(x, oa, ob, gain, w_merge, b_merge, w_proj_a, w_proj_b, w_out, tm):
    b, n, _ = x.shape
    tok = pl.BlockSpec((1, tm, D_MODEL), lambda bi, i: (bi, i, 0))
    consts = (gain, w_merge, b_merge, w_proj_a, w_proj_b, w_out)
    return pl.pallas_call(
        _out_kernel,
        grid=(b, n // tm),
        in_specs=[tok, tok, tok] + [_const_spec(c.shape) for c in consts],
        out_specs=tok,
        out_shape=jax.ShapeDtypeStruct(x.shape, F32),
        compiler_params=pltpu.CompilerParams(
            dimension_semantics=("arbitrary", "arbitrary"), vmem_limit_bytes=VMEM_LIMIT_BYTES),
        name="out_proj",
    )(x, oa, ob, *consts)


def _rope_tables(n, rot_dim):
    rows = n // GRID_W
    axis_dim = rot_dim // 2
    inv_freq = ROPE_THETA ** (-jnp.arange(0, axis_dim, 2, dtype=F32) / axis_dim)
    ang_r = jnp.arange(rows, dtype=F32)[:, None] * inv_freq[None, :]
    ang_c = jnp.arange(GRID_W, dtype=F32)[:, None] * inv_freq[None, :]
    cr, sr, cc, sc = jnp.cos(ang_r), jnp.sin(ang_r), jnp.cos(ang_c), jnp.sin(ang_c)
    zr, zc = jnp.zeros_like(cr), jnp.zeros_like(cc)
    pad_r, pad_c = (jnp.zeros((m, HALF - rot_dim // 2), F32) for m in (rows, GRID_W))
    by_row = jnp.stack([jnp.concatenate([cr, zr, pad_r, cr, zr, pad_r], axis=1),
                        jnp.concatenate([-sr, zr, pad_r, sr, zr, pad_r], axis=1)])
    by_col = jnp.stack([jnp.concatenate([zc, cc, pad_c, zc, cc, pad_c], axis=1),
                        jnp.concatenate([zc, -sc, pad_c, zc, sc, pad_c], axis=1)])
    return (by_row[:, :, None, :] + by_col[:, None, :, :]).reshape(2, n, LANES)


def _identity_rope(n):
    return jnp.stack([jnp.ones((n, LANES), F32), jnp.zeros((n, LANES), F32)])


def _gather_cols(w, idx):
    pieces, i = [], 0
    while i < len(idx):
        j = i + 1
        if idx[i] < 0:
            while j < len(idx) and idx[j] < 0:
                j += 1
            pieces.append(jnp.zeros((w.shape[0], j - i), w.dtype))
        else:
            while j < len(idx) and idx[j] == idx[j - 1] + 1:
                j += 1
            pieces.append(w[:, int(idx[i]):int(idx[j - 1]) + 1])
        i = j
    return jnp.concatenate(pieces, axis=1)


def _pad_meta(a):
    return jnp.pad(a[0], ((0, 0), (0, META_PAD - N_META), (0, 0)))


def _tile(n, pref):
    return pref if n % pref == 0 else n


def kernel(x_prompt, x_sample, meta_tokens, norm_gain, w_in, b_merge, q_norm_a, k_norm_a, cq_norm, ckv_norm,
           w_uq, w_ukv, q_norm_b, k_norm_b, w_proj_a, w_proj_b, w_out):
    assert norm_gain.shape[0] == 1, "meta-token queries are only skipped for a single layer"
    wi = w_in[0].astype(BF16)
    s0 = HEADS_A * HEAD_DIM_A
    s1 = s0 + 2 * KV_HEADS_A * HEAD_DIM_A
    s2 = s1 + HEADS_A * HEAD_DIM_A
    s3 = s2 + Q_LORA + KV_LORA
    s4 = s3 + ROPE_B
    rope_b_cols = np.where(LANES_ROPE_B < 0, -1, NOPE_B + LANES_ROPE_B)
    qk_a_cols = np.concatenate([h * HEAD_DIM_A + LANES_ROPE_A for h in range(HEADS_A + KV_HEADS_A)])
    qkv_cols = np.concatenate([qk_a_cols, np.arange(s0 + KV_HEADS_A * HEAD_DIM_A, s1), np.arange(s2, s3),
                               np.where(LANES_ROPE_B < 0, -1, s3 + LANES_ROPE_B)])
    w_qkv = _gather_cols(wi, qkv_cols)
    w_gates = jnp.concatenate([wi[:, s1:s2], wi[:, s4:s4 + D_MODEL]], axis=1)
    w_merge = wi[:, s4 + D_MODEL:]
    head_b_cols = np.concatenate([np.arange(NOPE_B), rope_b_cols])
    uq_cols = np.concatenate([np.where(head_b_cols < 0, -1, h * QK_B + head_b_cols) for h in range(HEADS_B)])
    w_uq_p = _gather_cols(w_uq[0].astype(BF16), uq_cols)
    w_ukv_b = w_ukv[0].astype(BF16)
    row = lambda a: a.reshape(1, -1).astype(F32)
    norms = (_gather_cols(row(q_norm_a[0]), LANES_ROPE_A), _gather_cols(row(k_norm_a[0]), LANES_ROPE_A),
             row(cq_norm[0]), row(ckv_norm[0]),
             _gather_cols(row(q_norm_b[0]), head_b_cols), _gather_cols(row(k_norm_b[0]), head_b_cols))
    gain = row(norm_gain[0])
    proj_w = (gain, w_qkv, w_uq_p, w_ukv_b) + norms + (w_gates,)
    wpa, wpb, wo = w_proj_a[0].astype(BF16), w_proj_b[0].astype(BF16), w_out[0].astype(BF16)
    bm = b_merge[0].astype(F32)

    meta = _project_qkv(meta_tokens[None].astype(F32), N_META, 1, *proj_w,
                        _identity_rope(N_META), _identity_rope(N_META))
    ka_m, va_m, kb_m, vb_m = (_pad_meta(meta[i]) for i in (1, 2, 4, 5))
    safe_a = _logits_bounded(q_norm_a[0], k_norm_a[0], HEAD_DIM_A)
    safe_b = _logits_bounded(q_norm_b[0], k_norm_b[0], QK_B)

    n_max = max(x_prompt.shape[1], x_sample.shape[1])
    rope_a, rope_b = _rope_tables(n_max, HEAD_DIM_A), _rope_tables(n_max, ROPE_B)

    def encode(x):
        n = x.shape[1]
        qa, ka, va, qb, kb, vb, gates = _project_qkv(x, _tile(n, 512), 2, *proj_w, rope_a, rope_b)
        oa = _attention(safe_a, qa, ka, va, ka_m, va_m, gates, group=GROUP_A, gate_col=0,
                        tq=_tile(n, 1024), n_sub=8, tk=_tile(n, 2048), tk_online=_tile(n, 512))
        tq_b = min(n, 4096)
        ob = _attention(safe_b, qb, kb, vb, kb_m, vb_m, gates, group=1, gate_col=D_MODEL,
                        tq=tq_b, n_sub=tq_b // 1024, tk=_tile(n, 2048), tk_online=_tile(n, 512))
        return _output(x, oa, ob, gain, w_merge, bm, wpa, wpb, wo, _tile(n, 512))

    return (encode(x_prompt), encode(x_sample))
```

```python
import functools

import numpy as np
import jax
import jax.numpy as jnp
from jax import lax
from jax.experimental import pallas as pl
from jax.experimental.pallas import tpu as pltpu

F32 = jnp.float32
BF16 = jnp.bfloat16

D_MODEL = 1024
N_META = 16
GRID_W = 64
ROPE_THETA = 10000.0
EPS = 1e-6

HEADS_A = 8
KV_HEADS_A = 2
HEAD_DIM_A = 128
GROUP_A = HEADS_A // KV_HEADS_A

HEADS_B = 8
Q_LORA = 384
KV_LORA = 256
NOPE_B = 128
ROPE_B = 64
V_B = 128
QK_B = NOPE_B + ROPE_B

LANES = 128
HALF = LANES // 2
QK_B_PAD = 2 * LANES
META_PAD = LANES
VMEM_LIMIT_BYTES = 56 * 1024 * 1024

C_QA = 0
C_KA = C_QA + HEADS_A * HEAD_DIM_A
C_VA = C_KA + KV_HEADS_A * HEAD_DIM_A
C_CQ = C_VA + KV_HEADS_A * HEAD_DIM_A
C_CKV = C_CQ + Q_LORA
C_KR = C_CKV + KV_LORA
C_QKV_END = C_KR + LANES
N_GATES = 2 * D_MODEL

NEG_BIG = -1e30
LOG2_E = 1.4426950408889634
MAX_UNSHIFTED_LOGIT = 60.0
UNROLLED_QUERY_ROWS = 8192

PROJ_ROWS = 512
PROJ_CHAINS = 2
GQA_ROWS = 1024
GQA_CHAINS = 8
MLA_ROWS = 4096
MLA_CHAIN_ROWS = 1024
KV_TILE = 2048
KV_TILE_ONLINE = 512


def _rotary_lanes(rot_dim):
    q = rot_dim // 4
    first = np.concatenate([np.arange(q), 2 * q + np.arange(q)])
    pad = np.full(HALF - 2 * q, -1)
    return np.concatenate([first, pad, first + q, pad])


LANES_ROPE_A = _rotary_lanes(HEAD_DIM_A)
LANES_ROPE_B = _rotary_lanes(ROPE_B)


def _rms_scale(x, dim):
    return lax.rsqrt(jnp.sum(x * x, axis=-1, keepdims=True) * (1.0 / dim) + EPS)


def _rope(y, tab_ref, rs):
    return y * tab_ref[0, rs, :] + pltpu.roll(y, HALF, 1) * tab_ref[1, rs, :]


def _normed_input(x):
    return x * _rms_scale(x, D_MODEL)


def _qkv_kernel(x_ref, gain_ref, w_in_ref, w_uq_ref, w_ukv_ref, qna_ref, kna_ref, cqn_ref, ckvn_ref,
                qnb_ref, knb_ref, wg_ref, rope_a_ref, rope_b_ref,
                qa_ref, ka_ref, va_ref, qb_ref, kb_ref, vb_ref, gates_ref, *, n_sub):
    tms = x_ref.shape[1] // n_sub
    gq = qna_ref[...] * (HEAD_DIM_A ** -0.5 * LOG2_E)
    gqb = qnb_ref[...] * (QK_B ** -0.5 * LOG2_E)
    gk_nope = knb_ref[:, :LANES]
    gk_rope = knb_ref[:, LANES:]

    for t in range(n_sub):
        rs = pl.ds(t * tms, tms)
        u = (_normed_input(x_ref[0, rs, :]) * gain_ref[...]).astype(BF16)

        def proj(c0, c1):
            return jnp.dot(u, w_in_ref[:, c0:c1], preferred_element_type=F32)

        zq = proj(C_QA, C_KA)
        for h in range(HEADS_A):
            xh = zq[:, h * LANES:(h + 1) * LANES]
            y = xh * _rms_scale(xh, HEAD_DIM_A) * gq
            qa_ref[0, h, rs, :] = _rope(y, rope_a_ref, rs).astype(BF16)
        zk = proj(C_KA, C_VA)
        for h in range(KV_HEADS_A):
            xh = zk[:, h * LANES:(h + 1) * LANES]
            y = xh * _rms_scale(xh, HEAD_DIM_A) * kna_ref[...]
            ka_ref[0, h, rs, :] = _rope(y, rope_a_ref, rs).astype(BF16)
        zv = proj(C_VA, C_CQ)
        for h in range(KV_HEADS_A):
            va_ref[0, h, rs, :] = zv[:, h * LANES:(h + 1) * LANES].astype(BF16)


        cq = proj(C_CQ, C_CKV)
        eps_q = EPS * (jnp.sum(cq * cq, axis=-1, keepdims=True) * (1.0 / Q_LORA) + EPS)
        zqb = jnp.dot((cq * cqn_ref[...]).astype(BF16), w_uq_ref[...], preferred_element_type=F32)
        for h in range(HEADS_B):
            xh = zqb[:, h * QK_B_PAD:(h + 1) * QK_B_PAD]
            y = xh * lax.rsqrt(jnp.sum(xh * xh, axis=-1, keepdims=True) * (1.0 / QK_B) + eps_q) * gqb
            qb_ref[0, h, rs, :LANES] = y[:, :LANES].astype(BF16)
            qb_ref[0, h, rs, LANES:] = _rope(y[:, LANES:], rope_b_ref, rs).astype(BF16)

        ckv = proj(C_CKV, C_KR)
        r_kv = _rms_scale(ckv, KV_LORA)
        zkv = jnp.dot((ckv * ckvn_ref[...]).astype(BF16), w_ukv_ref[...], preferred_element_type=F32)
        kr = proj(C_KR, C_QKV_END)
        kr_ss = jnp.sum(kr * kr, axis=-1, keepdims=True)
        kr_rot = _rope(kr * gk_rope, rope_b_ref, rs)
        for h in range(HEADS_B):
            kn = zkv[:, h * 2 * LANES:h * 2 * LANES + LANES]
            ss = jnp.sum(kn * kn, axis=-1, keepdims=True) * (r_kv * r_kv) + kr_ss
            r = lax.rsqrt(ss * (1.0 / QK_B) + EPS)
            kb_ref[0, h, rs, :LANES] = (kn * (r * r_kv) * gk_nope).astype(BF16)
            kb_ref[0, h, rs, LANES:] = (kr_rot * r).astype(BF16)
            vb_ref[0, h, rs, :] = (zkv[:, h * 2 * LANES + LANES:(h + 1) * 2 * LANES] * r_kv).astype(BF16)

        gates_ref[0, rs, :] = jnp.dot(u, wg_ref[...], preferred_element_type=F32)


def _const_spec(shape):
    return pl.BlockSpec(shape, lambda *_: (0,) * len(shape))


def _project_qkv(x, tm, n_sub, gain, w_qkv, w_uq, w_ukv, qna, kna, cqn, ckvn, qnb, knb, wg, rope_a, rope_b):
    b, n, _ = x.shape

    def head_out(heads, width):
        return (jax.ShapeDtypeStruct((b, heads, n, width), BF16),
                pl.BlockSpec((1, heads, tm, width), lambda bi, i: (bi, 0, i, 0)))

    outs = [head_out(HEADS_A, HEAD_DIM_A), head_out(KV_HEADS_A, HEAD_DIM_A), head_out(KV_HEADS_A, HEAD_DIM_A),
            head_out(HEADS_B, QK_B_PAD), head_out(HEADS_B, QK_B_PAD), head_out(HEADS_B, V_B),
            (jax.ShapeDtypeStruct((b, n, N_GATES), F32), pl.BlockSpec((1, tm, N_GATES), lambda bi, i: (bi, i, 0)))]
    consts = (gain, w_qkv, w_uq, w_ukv, qna, kna, cqn, ckvn, qnb, knb, wg)
    rope_spec = pl.BlockSpec((2, tm, LANES), lambda bi, i: (0, i, 0))
    return pl.pallas_call(
        functools.partial(_qkv_kernel, n_sub=n_sub),
        grid=(b, n // tm),
        in_specs=[pl.BlockSpec((1, tm, D_MODEL), lambda bi, i: (bi, i, 0))]
        + [_const_spec(c.shape) for c in consts] + [rope_spec, rope_spec],
        out_specs=[o[1] for o in outs],
        out_shape=[o[0] for o in outs],
        compiler_params=pltpu.CompilerParams(
            dimension_semantics=("arbitrary", "arbitrary"), vmem_limit_bytes=VMEM_LIMIT_BYTES),
        name="qkv_proj",
    )(x, *consts, rope_a, rope_b)


def _lane_block_sum(p):
    out = p[:, :LANES]
    for c in range(1, p.shape[1] // LANES):
        out = out + p[:, c * LANES:(c + 1) * LANES]
    return out


def _attn_kernel(safe_ref, q_ref, k_ref, v_ref, km_ref, vm_ref, g_ref, o_ref, m_sc, l_sc, acc_sc,
                 *, group, tq, n_sub, tk, tk_online):
    tqs = tq // n_sub
    ms = group * tqs
    n = k_ref.shape[2]
    nt = (((1,), (1,)), ((), ()))
    subs = range(n_sub)
    rows = [pl.ds(s * ms, ms) for s in subs]
    is_meta = lax.broadcasted_iota(jnp.int32, (ms, META_PAD), 1) < N_META

    def q_sub(s):
        return q_ref[0, :, s * tqs:(s + 1) * tqs, :].reshape(ms, q_ref.shape[-1])

    def meta_logits(s):
        return lax.dot_general(q_sub(s), km_ref[0], nt, preferred_element_type=F32)

    def kv_tile(j, width):
        off = pl.multiple_of(j * width, width)
        return k_ref[0, 0, pl.ds(off, width), :], v_ref[0, 0, pl.ds(off, width), :]

    def finish(s, l):
        o = acc_sc[rows[s], :] / l
        for h in range(group):
            g = g_ref[0, s * tqs:(s + 1) * tqs, h * LANES:(h + 1) * LANES]
            o_ref[0, s * tqs:(s + 1) * tqs, h * LANES:(h + 1) * LANES] = (
                o[h * tqs:(h + 1) * tqs] * (g * jax.nn.sigmoid(g))).astype(BF16)

    @pl.when(safe_ref[0] != 0)
    def _():
        for s in subs:
            p = jnp.where(is_meta, jnp.exp2(meta_logits(s)), 0.0)
            l_sc[rows[s], :] = p
            acc_sc[rows[s], :] = jnp.dot(p.astype(BF16), vm_ref[0], preferred_element_type=F32)

        def body(j, carry):
            k, v = kv_tile(j, tk)
            for s in subs:
                p = jnp.exp2(lax.dot_general(q_sub(s), k, nt, preferred_element_type=F32))
                l_sc[rows[s], :] += _lane_block_sum(p)
                acc_sc[rows[s], :] += jnp.dot(p.astype(BF16), v, preferred_element_type=F32)
            return carry

        lax.fori_loop(0, n // tk, body, 0, unroll=min(n // tk, max(1, UNROLLED_QUERY_ROWS // (n_sub * ms))))
        for s in subs:
            finish(s, jnp.sum(l_sc[rows[s], :], axis=1, keepdims=True))

    @pl.when(safe_ref[0] == 0)
    def _():
        for s in subs:
            sm = jnp.where(is_meta, meta_logits(s), NEG_BIG)
            m0 = jnp.max(sm, axis=1, keepdims=True)
            p = jnp.exp2(sm - m0)
            m_sc[rows[s], :] = jnp.broadcast_to(m0, (ms, LANES))
            l_sc[rows[s], :] = jnp.broadcast_to(jnp.sum(p, axis=1, keepdims=True), (ms, LANES))
            acc_sc[rows[s], :] = jnp.dot(p.astype(BF16), vm_ref[0], preferred_element_type=F32)

        def body(j, carry):
            k, v = kv_tile(j, tk_online)
            for s in subs:
                sc = lax.dot_general(q_sub(s), k, nt, preferred_element_type=F32)
                m_prev = m_sc[rows[s], :]
                m_new = jnp.maximum(m_prev, jnp.max(sc, axis=1, keepdims=True))
                alpha = jnp.exp2(m_prev - m_new)
                p = jnp.exp2(sc - jnp.concatenate([m_new] * (tk_online // LANES), axis=1))
                l_sc[rows[s], :] = alpha * l_sc[rows[s], :] + jnp.sum(p, axis=1, keepdims=True)
                acc_sc[rows[s], :] = alpha * acc_sc[rows[s], :] + jnp.dot(
                    p.astype(BF16), v, preferred_element_type=F32)
                m_sc[rows[s], :] = m_new
            return carry

        lax.fori_loop(0, n // tk_online, body, 0)
        for s in subs:
            finish(s, l_sc[rows[s], :])


def _attention(safe, q, k, v, k_meta, v_meta, gates, *, group, gate_col, tq, n_sub, tk, tk_online):
    b, hq, n, dk = q.shape
    hkv = k.shape[1]
    dv = v.shape[-1]
    width = group * dv
    assert n % tq == 0 and tq % n_sub == 0 and n % tk == 0 and n % tk_online == 0, (n, tq, n_sub, tk, tk_online)
    gate_blk = gate_col // width
    kernel = functools.partial(_attn_kernel, group=group, tq=tq, n_sub=n_sub, tk=tk, tk_online=tk_online)
    grid_spec = pltpu.PrefetchScalarGridSpec(
        num_scalar_prefetch=1,
        grid=(b, hkv, n // tq),
        in_specs=[
            pl.BlockSpec((1, group, tq, dk), lambda bi, g, i, s: (bi, g, i, 0)),
            pl.BlockSpec((1, 1, n, dk), lambda bi, g, i, s: (bi, g, 0, 0)),
            pl.BlockSpec((1, 1, n, dv), lambda bi, g, i, s: (bi, g, 0, 0)),
            pl.BlockSpec((1, META_PAD, dk), lambda bi, g, i, s: (g, 0, 0)),
            pl.BlockSpec((1, META_PAD, dv), lambda bi, g, i, s: (g, 0, 0)),
            pl.BlockSpec((1, tq, width), lambda bi, g, i, s: (bi, i, gate_blk + g)),
        ],
        out_specs=pl.BlockSpec((1, tq, width), lambda bi, g, i, s: (bi, i, g)),
        scratch_shapes=[pltpu.VMEM((group * tq, LANES), F32)] * 2 + [pltpu.VMEM((group * tq, dv), F32)],
    )
    return pl.pallas_call(
        kernel,
        grid_spec=grid_spec,
        out_shape=jax.ShapeDtypeStruct((b, n, hq * dv), BF16),
        compiler_params=pltpu.CompilerParams(
            dimension_semantics=("arbitrary", "arbitrary", "arbitrary"), vmem_limit_bytes=VMEM_LIMIT_BYTES),
        name="attn_g%d" % group,
    )(safe, q, k, v, k_meta, v_meta, gates)


def _logits_bounded(g_q, g_k, head_dim):
    bound = jnp.max(jnp.abs(g_q)) * jnp.max(jnp.abs(g_k)) * (head_dim ** 0.5)
    return (bound <= MAX_UNSHIFTED_LOGIT).astype(jnp.int32).reshape(1)


def _out_kernel(x_ref, oa_ref, ob_ref, gain_ref, wm_ref, bm_ref, wpa_ref, wpb_ref, wo_ref, y_ref):
    x = x_ref[0]
    u = (_normed_input(x) * gain_ref[...]).astype(BF16)
    m = jnp.dot(u, wm_ref[...], preferred_element_type=F32)
    ya = jnp.dot(oa_ref[0], wpa_ref[...], preferred_element_type=F32)
    yb = jnp.dot(ob_ref[0], wpb_ref[...], preferred_element_type=F32)
    mix = (jax.nn.sigmoid(m[:, :D_MODEL] + bm_ref[0:1]) * ya
           + jax.nn.sigmoid(m[:, D_MODEL:] + bm_ref[1:2]) * yb)
    y_ref[0] = x + jnp.dot(mix.astype(BF16), wo_ref[...], preferred_element_type=F32)


def _output(x, oa, ob, gain, w_merge, b_merge, w_proj_a, w_proj_b, w_out, tm):
    b, n, _ = x.shape
    tok = pl.BlockSpec((1, tm, D_MODEL), lambda bi, i: (bi, i, 0))
    consts = (gain, w_merge, b_merge, w_proj_a, w_proj_b, w_out)
    return pl.pallas_call(
        _out_kernel,
        grid=(b, n // tm),
        in_specs=[tok, tok, tok] + [_const_spec(c.shape) for c in consts],
        out_specs=tok,
        out_shape=jax.ShapeDtypeStruct(x.shape, F32),
        compiler_params=pltpu.CompilerParams(
            dimension_semantics=("arbitrary", "arbitrary"), vmem_limit_bytes=VMEM_LIMIT_BYTES),
        name="out_proj",
    )(x, oa, ob, *consts)


def _rope_tables(n, rot_dim):
    rows = n // GRID_W
    axis_dim = rot_dim // 2
    inv_freq = ROPE_THETA ** (-jnp.arange(0, axis_dim, 2, dtype=F32) / axis_dim)
    ang_r = jnp.arange(rows, dtype=F32)[:, None] * inv_freq[None, :]
    ang_c = jnp.arange(GRID_W, dtype=F32)[:, None] * inv_freq[None, :]
    cr, sr, cc, sc = jnp.cos(ang_r), jnp.sin(ang_r), jnp.cos(ang_c), jnp.sin(ang_c)
    zr, zc = jnp.zeros_like(cr), jnp.zeros_like(cc)
    pad_r, pad_c = (jnp.zeros((m, HALF - rot_dim // 2), F32) for m in (rows, GRID_W))
    by_row = jnp.stack([jnp.concatenate([cr, zr, pad_r, cr, zr, pad_r], axis=1),
                        jnp.concatenate([-sr, zr, pad_r, sr, zr, pad_r], axis=1)])
    by_col = jnp.stack([jnp.concatenate([zc, cc, pad_c, zc, cc, pad_c], axis=1),
                        jnp.concatenate([zc, -sc, pad_c, zc, sc, pad_c], axis=1)])
    return (by_row[:, :, None, :] + by_col[:, None, :, :]).reshape(2, n, LANES)


def _identity_rope(n):
    return jnp.stack([jnp.ones((n, LANES), F32), jnp.zeros((n, LANES), F32)])


def _gather_cols(w, idx):
    pieces, i = [], 0
    while i < len(idx):
        j = i + 1
        if idx[i] < 0:
            while j < len(idx) and idx[j] < 0:
                j += 1
            pieces.append(jnp.zeros((w.shape[0], j - i), w.dtype))
        else:
            while j < len(idx) and idx[j] == idx[j - 1] + 1:
                j += 1
            pieces.append(w[:, int(idx[i]):int(idx[j - 1]) + 1])
        i = j
    return jnp.concatenate(pieces, axis=1)


def _pad_meta(a):
    return jnp.pad(a[0], ((0, 0), (0, META_PAD - N_META), (0, 0)))


def _tile(n, pref):
    return pref if n % pref == 0 else n


def kernel(x_prompt, x_sample, meta_tokens, norm_gain, w_in, b_merge, q_norm_a, k_norm_a, cq_norm, ckv_norm,
           w_uq, w_ukv, q_norm_b, k_norm_b, w_proj_a, w_proj_b, w_out):
    assert norm_gain.shape[0] == 1, "meta-token queries are only skipped for a single layer"
    wi = w_in[0].astype(BF16)
    s0 = HEADS_A * HEAD_DIM_A
    s1 = s0 + 2 * KV_HEADS_A * HEAD_DIM_A
    s2 = s1 + HEADS_A * HEAD_DIM_A
    s3 = s2 + Q_LORA + KV_LORA
    s4 = s3 + ROPE_B
    rope_b_cols = np.where(LANES_ROPE_B < 0, -1, NOPE_B + LANES_ROPE_B)
    qk_a_cols = np.concatenate([h * HEAD_DIM_A + LANES_ROPE_A for h in range(HEADS_A + KV_HEADS_A)])
    qkv_cols = np.concatenate([qk_a_cols, np.arange(s0 + KV_HEADS_A * HEAD_DIM_A, s1), np.arange(s2, s3),
                               np.where(LANES_ROPE_B < 0, -1, s3 + LANES_ROPE_B)])
    w_qkv = _gather_cols(wi, qkv_cols)
    w_gates = jnp.concatenate([wi[:, s1:s2], wi[:, s4:s4 + D_MODEL]], axis=1)
    w_merge = wi[:, s4 + D_MODEL:]
    head_b_cols = np.concatenate([np.arange(NOPE_B), rope_b_cols])
    uq_cols = np.concatenate([np.where(head_b_cols < 0, -1, h * QK_B + head_b_cols) for h in range(HEADS_B)])
    w_uq_p = _gather_cols(w_uq[0].astype(BF16), uq_cols)
    w_ukv_b = w_ukv[0].astype(BF16)
    row = lambda a: a.reshape(1, -1).astype(F32)
    norms = (_gather_cols(row(q_norm_a[0]), LANES_ROPE_A), _gather_cols(row(k_norm_a[0]), LANES_ROPE_A),
             row(cq_norm[0]), row(ckv_norm[0]),
             _gather_cols(row(q_norm_b[0]), head_b_cols), _gather_cols(row(k_norm_b[0]), head_b_cols))
    gain = row(norm_gain[0])
    proj_w = (gain, w_qkv, w_uq_p, w_ukv_b) + norms + (w_gates,)
    wpa, wpb, wo = w_proj_a[0].astype(BF16), w_proj_b[0].astype(BF16), w_out[0].astype(BF16)
    bm = b_merge[0].astype(F32)

    meta = _project_qkv(meta_tokens[None].astype(F32), N_META, 1, *proj_w,
                        _identity_rope(N_META), _identity_rope(N_META))
    ka_m, va_m, kb_m, vb_m = (_pad_meta(meta[i]) for i in (1, 2, 4, 5))
    safe_a = _logits_bounded(q_norm_a[0], k_norm_a[0], HEAD_DIM_A)
    safe_b = _logits_bounded(q_norm_b[0], k_norm_b[0], QK_B)

    n_max = max(x_prompt.shape[1], x_sample.shape[1])
    rope_a, rope_b = _rope_tables(n_max, HEAD_DIM_A), _rope_tables(n_max, ROPE_B)

    def encode(x):
        n = x.shape[1]
        qa, ka, va, qb, kb, vb, gates = _project_qkv(x, _tile(n, PROJ_ROWS), PROJ_CHAINS, *proj_w, rope_a, rope_b)
        oa = _attention(safe_a, qa, ka, va, ka_m, va_m, gates, group=GROUP_A, gate_col=0,
                        tq=_tile(n, GQA_ROWS), n_sub=GQA_CHAINS, tk=_tile(n, KV_TILE),
                        tk_online=_tile(n, KV_TILE_ONLINE))
        tq_b = min(n, MLA_ROWS)
        ob = _attention(safe_b, qb, kb, vb, kb_m, vb_m, gates, group=1, gate_col=D_MODEL,
                        tq=tq_b, n_sub=tq_b // MLA_CHAIN_ROWS, tk=_tile(n, KV_TILE),
                        tk_online=_tile(n, KV_TILE_ONLINE))
        return _output(x, oa, ob, gain, w_merge, bm, wpa, wpb, wo, _tile(n, PROJ_ROWS))

    return (encode(x_prompt), encode(x_sample))
```
